```python
import math
import jax, jax.numpy as jnp
from jax import lax
import numpy as np

D_MODEL = 2048
BATCH = 2
SEQ = 8192
DEPTH = 1

FOX_HEADS = 8
FOX_HEAD_DIM = 128
Q_BLOCK = 128
GDN_HEADS = 8
GDN_HEAD_DIM = 128
CONV_WIDTH = 4
GDN_CHUNK = 64
N_EXPERTS = 32
TOP_K = 4
D_FF = D_MODEL
SWIGLU_LIMIT = 7.0
SWIGLU_ALPHA = 1.702
MOE_BLOCK = 128
RMS_EPS = 1e-6

FOX_W = FOX_HEADS * FOX_HEAD_DIM
GDN_W = GDN_HEADS * GDN_HEAD_DIM
SPLITS = (FOX_W, FOX_W, FOX_W, FOX_HEADS, 3 * GDN_W, GDN_W, GDN_HEADS, GDN_HEADS, D_MODEL, D_MODEL)
IN_COLS = 3 * FOX_W + FOX_HEADS + 4 * GDN_W + 2 * GDN_HEADS + 2 * D_MODEL

kernel_name = "fox_gdn_gated_hybrid_moe_layer"


def rms_norm(x, w):
    xf = x.astype(jnp.float32)
    y = xf * lax.rsqrt(jnp.mean(xf * xf, axis=-1, keepdims=True) + RMS_EPS)
    return (y * w.astype(jnp.float32)).astype(x.dtype)


def l2_normalize(x):
    return x * lax.rsqrt(jnp.sum(x * x, axis=-1, keepdims=True) + RMS_EPS)


def fox_attention(q, k, v, f_logit, q_norm_w, k_norm_w):
    B, S, H, Dh = q.shape
    qf = rms_norm(q, q_norm_w).astype(jnp.float32) * (Dh ** -0.5)
    kf = rms_norm(k, k_norm_w).astype(jnp.float32)
    vf = v.astype(jnp.float32)
    c = jnp.cumsum(jax.nn.log_sigmoid(f_logit.astype(jnp.float32)), axis=1)
    c_k = jnp.transpose(c, (0, 2, 1))
    nblk = S // Q_BLOCK
    qb = jnp.moveaxis(qf.reshape(B, nblk, Q_BLOCK, H, Dh), 1, 0)
    cb = jnp.moveaxis(c.reshape(B, nblk, Q_BLOCK, H), 1, 0)
    k_pos = jnp.arange(S)

    def one_block(args):
        i, q_i, c_i = args
        s = jnp.einsum('bqhd,bkhd->bhqk', q_i, kf)
        s = s + jnp.transpose(c_i, (0, 2, 1))[..., :, None] - c_k[..., None, :]
        q_pos = i * Q_BLOCK + jnp.arange(Q_BLOCK)
        causal = q_pos[:, None] >= k_pos[None, :]
        s = jnp.where(causal, s, -jnp.inf)
        p = jax.nn.softmax(s, axis=-1)
        return jnp.einsum('bhqk,bkhd->bqhd', p, vf)

    o = lax.map(one_block, (jnp.arange(nblk), qb, cb))
    return jnp.moveaxis(o, 0, 1).reshape(B, S, H * Dh).astype(v.dtype)


def causal_depthwise_conv(x, w):
    K, C = w.shape
    return lax.conv_general_dilated(
        x, w[:, None, :].astype(x.dtype), window_strides=(1,), padding=[(K - 1, 0)],
        dimension_numbers=('NWC', 'WIO', 'NWC'), feature_group_count=C)


def gated_delta_rule_chunked(q, k, v, g, beta):
    B, S, H, Dk = q.shape
    Dv = v.shape[-1]
    C = GDN_CHUNK
    N = S // C

    def to_chunks(t):
        return jnp.moveaxis(t.reshape((B, N, C, H) + t.shape[3:]), 3, 1)

    qc = to_chunks(q) * (Dk ** -0.5)
    kc = to_chunks(k)
    vc = to_chunks(v)
    bc = to_chunks(beta)
    gc = jnp.cumsum(to_chunks(g), axis=-1)
    idx = jnp.arange(C)
    incl = idx[:, None] >= idx[None, :]
    strict = idx[:, None] > idx[None, :]
    decay = jnp.exp(jnp.where(incl, gc[..., :, None] - gc[..., None, :], -jnp.inf))
    k_beta = kc * bc[..., None]
    v_beta = vc * bc[..., None]
    L = jnp.where(strict, jnp.einsum('bhnid,bhnjd->bhnij', k_beta, kc) * decay, 0.0)
    A = L + jnp.eye(C, dtype=L.dtype)
    rhs = jnp.concatenate([v_beta, k_beta * jnp.exp(gc)[..., None]], axis=-1)
    sol = lax.linalg.triangular_solve(A, rhs, left_side=True, lower=True, unit_diagonal=True)
    u, w = sol[..., :Dv], sol[..., Dv:]
    attn = jnp.einsum('bhnid,bhnjd->bhnij', qc, kc) * decay
    q_dec = qc * jnp.exp(gc)[..., None]
    k_dec = kc * jnp.exp(gc[..., -1:] - gc)[..., None]
    g_last = jnp.exp(gc[..., -1])
    xs = (jnp.moveaxis(u, 2, 0), jnp.moveaxis(w, 2, 0), jnp.moveaxis(attn, 2, 0),
          jnp.moveaxis(q_dec, 2, 0), jnp.moveaxis(k_dec, 2, 0), jnp.moveaxis(g_last, 2, 0))

    def step(state, inp):
        u_n, w_n, attn_n, q_n, k_n, gl_n = inp
        v_new = u_n - jnp.einsum('bhck,bhkv->bhcv', w_n, state)
        o_n = jnp.einsum('bhck,bhkv->bhcv', q_n, state) + jnp.einsum('bhij,bhjv->bhiv', attn_n, v_new)
        state = state * gl_n[..., None, None] + jnp.einsum('bhck,bhcv->bhkv', k_n, v_new)
        return state, o_n

    state0 = jnp.zeros((B, H, Dk, Dv), jnp.float32)
    _, o = lax.scan(step, state0, xs)
    o = jnp.moveaxis(jnp.moveaxis(o, 0, 2), 1, 3)
    return o.reshape(B, S, H, Dv)


def gdn_branch(qkv_raw, z, beta_logit, alpha_logit, conv_w, A_log, dt_bias, out_norm_w):
    B, S, _ = qkv_raw.shape
    H, Dh = GDN_HEADS, GDN_HEAD_DIM
    qkv = jax.nn.silu(causal_depthwise_conv(qkv_raw, conv_w)).astype(jnp.float32)
    q, k, v = jnp.split(qkv, 3, axis=-1)
    q = l2_normalize(q.reshape(B, S, H, Dh))
    k = l2_normalize(k.reshape(B, S, H, Dh))
    v = v.reshape(B, S, H, Dh)
    beta = jax.nn.sigmoid(beta_logit.astype(jnp.float32))
    g = -jnp.exp(A_log.astype(jnp.float32)) * jax.nn.softplus(alpha_logit.astype(jnp.float32) + dt_bias.astype(jnp.float32))
    o = gated_delta_rule_chunked(q, k, v, g, beta)
    o = rms_norm(o, out_norm_w) * jax.nn.silu(z.reshape(B, S, H, Dh).astype(jnp.float32))
    return o.reshape(B, S, H * Dh).astype(qkv_raw.dtype)


def moe_ffn(x, w_router, b_router, w_gate_up, b_gate_up, w_down, b_down):
    B, S, D = x.shape
    T = B * S
    xt = x.reshape(T, D)
    logits = (xt @ w_router + b_router).astype(jnp.float32)
    top_val, top_idx = lax.top_k(logits, TOP_K)
    top_w = jax.nn.softmax(top_val, axis=-1).astype(x.dtype)
    A = T * TOP_K
    e_flat = top_idx.reshape(A).astype(jnp.int32)
    tok_flat = jnp.arange(A, dtype=jnp.int32) // TOP_K
    w_flat = top_w.reshape(A)
    order = jnp.argsort(e_flat)
    e_sorted = e_flat[order]
    counts = jnp.bincount(e_flat, length=N_EXPERTS).astype(jnp.int32)
    starts = jnp.cumsum(counts) - counts
    padded = (counts + MOE_BLOCK - 1) // MOE_BLOCK * MOE_BLOCK
    pad_end = jnp.cumsum(padded)
    pad_start = pad_end - padded
    rank = jnp.arange(A, dtype=jnp.int32) - starts[e_sorted]
    dest = pad_start[e_sorted] + rank
    n_blocks = -(-A // MOE_BLOCK) + N_EXPERTS
    P = n_blocks * MOE_BLOCK
    slot_tok = jnp.full((P,), T, jnp.int32).at[dest].set(tok_flat[order])
    slot_w = jnp.zeros((P,), x.dtype).at[dest].set(w_flat[order])
    block_expert = jnp.minimum(
        jnp.searchsorted(pad_end, jnp.arange(n_blocks, dtype=jnp.int32) * MOE_BLOCK, side='right'),
        N_EXPERTS - 1).astype(jnp.int32)
    x_pad = jnp.concatenate([xt, jnp.zeros((1, D), xt.dtype)], axis=0)

    def expert_block(args):
        e, tok = args
        h = x_pad[tok] @ w_gate_up[e] + b_gate_up[e]
        gate = jnp.minimum(h[:, 0::2], SWIGLU_LIMIT)
        lin = jnp.clip(h[:, 1::2], -SWIGLU_LIMIT, SWIGLU_LIMIT)
        act = gate * jax.nn.sigmoid(SWIGLU_ALPHA * gate) * (lin + 1.0)
        return act @ w_down[e] + b_down[e]

    y = lax.map(expert_block, (block_expert, slot_tok.reshape(n_blocks, MOE_BLOCK)))
    y = y.reshape(P, D) * slot_w[:, None]
    out = jax.ops.segment_sum(y, slot_tok, num_segments=T + 1)[:T]
    return out.reshape(B, S, D).astype(x.dtype)


def setup_inputs(seed: int = 0) -> dict:
    key = jax.random.key(seed)
    ks = jax.random.split(key, 24)
    f32 = jnp.float32
    L = DEPTH

    def nrm(k, shape, scale):
        return jax.random.normal(k, shape, f32) * scale

    x = jax.random.normal(ks[0], (BATCH, SEQ, D_MODEL), f32)
    norm1_w = 1.0 + nrm(ks[1], (L, D_MODEL), 0.02)
    w_in = nrm(ks[2], (L, D_MODEL, IN_COLS), D_MODEL ** -0.5)
    b_fox_f = jnp.linspace(2.0, 6.0, FOX_HEADS, dtype=f32)[None, :] + nrm(ks[3], (L, FOX_HEADS), 0.1)
    fox_q_norm_w = 1.0 + nrm(ks[4], (L, FOX_HEAD_DIM), 0.02)
    fox_k_norm_w = 1.0 + nrm(ks[5], (L, FOX_HEAD_DIM), 0.02)
    gdn_conv_w = nrm(ks[6], (L, CONV_WIDTH, 3 * GDN_W), CONV_WIDTH ** -0.5)
    gdn_A_log = jnp.log(jax.random.uniform(ks[7], (L, GDN_HEADS), f32, 1.0, 16.0))
    dt = jnp.exp(jax.random.uniform(ks[8], (L, GDN_HEADS), f32, math.log(1e-3), math.log(1e-1)))
    gdn_dt_bias = dt + jnp.log(-jnp.expm1(-dt))
    gdn_out_norm_w = 1.0 + nrm(ks[9], (L, GDN_HEAD_DIM), 0.02)
    w_up_fox = nrm(ks[10], (L, FOX_W, D_MODEL), FOX_W ** -0.5)
    w_up_gdn = nrm(ks[11], (L, GDN_W, D_MODEL), GDN_W ** -0.5)
    w_o = nrm(ks[12], (L, D_MODEL, D_MODEL), D_MODEL ** -0.5)
    norm2_w = 1.0 + nrm(ks[13], (L, D_MODEL), 0.02)
    w_router = nrm(ks[14], (L, D_MODEL, N_EXPERTS), D_MODEL ** -0.5)
    b_router = nrm(ks[15], (L, N_EXPERTS), 0.01)
    w_gate_up = nrm(ks[16], (L, N_EXPERTS, D_MODEL, 2 * D_FF), D_MODEL ** -0.5)
    b_gate_up = nrm(ks[17], (L, N_EXPERTS, 2 * D_FF), 0.01)
    w_down = nrm(ks[18], (L, N_EXPERTS, D_FF, D_MODEL), D_FF ** -0.5)
    b_down = nrm(ks[19], (L, N_EXPERTS, D_MODEL), 0.01)
    return {"x": x, "norm1_w": norm1_w, "w_in": w_in, "b_fox_f": b_fox_f,
            "fox_q_norm_w": fox_q_norm_w, "fox_k_norm_w": fox_k_norm_w,
            "gdn_conv_w": gdn_conv_w, "gdn_A_log": gdn_A_log, "gdn_dt_bias": gdn_dt_bias,
            "gdn_out_norm_w": gdn_out_norm_w, "w_up_fox": w_up_fox, "w_up_gdn": w_up_gdn,
            "w_o": w_o, "norm2_w": norm2_w, "w_router": w_router, "b_router": b_router,
            "w_gate_up": w_gate_up, "b_gate_up": b_gate_up, "w_down": w_down, "b_down": b_down}


def reference(x, norm1_w, w_in, b_fox_f, fox_q_norm_w, fox_k_norm_w, gdn_conv_w, gdn_A_log,
              gdn_dt_bias, gdn_out_norm_w, w_up_fox, w_up_gdn, w_o, norm2_w, w_router, b_router,
              w_gate_up, b_gate_up, w_down, b_down):
    B, S, _ = x.shape
    split_at = [int(i) for i in np.cumsum(SPLITS)[:-1]]
    for l in range(DEPTH):
        h = rms_norm(x, norm1_w[l])
        proj = h @ w_in[l]
        fq, fk, fv, ff, gqkv, gz, gb, ga, gate_a, gate_b = jnp.split(proj, split_at, axis=-1)
        y_fox = fox_attention(
            fq.reshape(B, S, FOX_HEADS, FOX_HEAD_DIM), fk.reshape(B, S, FOX_HEADS, FOX_HEAD_DIM),
            fv.reshape(B, S, FOX_HEADS, FOX_HEAD_DIM), ff + b_fox_f[l],
            fox_q_norm_w[l], fox_k_norm_w[l])
        y_gdn = gdn_branch(gqkv, gz, gb, ga, gdn_conv_w[l], gdn_A_log[l], gdn_dt_bias[l],
                           gdn_out_norm_w[l])
        merged = (jax.nn.sigmoid(gate_a) * (y_fox @ w_up_fox[l])
                  + jax.nn.sigmoid(gate_b) * (y_gdn @ w_up_gdn[l]))
        x = x + merged @ w_o[l]
        x = x + moe_ffn(rms_norm(x, norm2_w[l]), w_router[l], b_router[l], w_gate_up[l],
                        b_gate_up[l], w_down[l], b_down[l])
    return x
```

```python
import functools

import jax
import jax.numpy as jnp
from jax import lax
from jax.experimental import pallas as pl
from jax.experimental.pallas import tpu as pltpu

F32 = jnp.float32
BF16 = jnp.bfloat16

N_HEADS = 8
HEAD_DIM = 128
HEADS_W = N_HEADS * HEAD_DIM
GDN_CHUNK = 64
CONV_WIDTH = 4
N_EXPERTS = 32
TOP_K = 4
SWIGLU_LIMIT = 7.0
SWIGLU_ALPHA = 1.702
RMS_EPS = 1e-6
LANES = 128
NEG_BIG = -1e30

VMEM_LIMIT = 56 * 1024 * 1024

MOE_ROWS = 512


def _cparams(sem):
    return pltpu.CompilerParams(dimension_semantics=sem, vmem_limit_bytes=VMEM_LIMIT)


def _nt_dot(a, b):
    return lax.dot_general(a, b, (((1,), (1,)), ((), ())), preferred_element_type=F32)


def _dot(a, b):
    return jnp.dot(a, b, preferred_element_type=F32)


def _dot_exact(a, b):
    return jnp.dot(a, b, preferred_element_type=F32, precision=lax.Precision.HIGHEST)


def _silu(x):
    return x * jax.nn.sigmoid(x)


BLK_GATE_A, BLK_GATE_B = 0, 2
BLK_FQ, BLK_FK, BLK_FV = 4, 5, 6
BLK_GQ, BLK_GK, BLK_GV, BLK_GZ = 7, 8, 9, 10
CHUNK_SHIFT = GDN_CHUNK.bit_length() - 1


def _in_proj_kernel(x_ref, n1_ref, w_ref, ws_ref, qn_ref, kn_ref, proj_ref, small_ref, h_scr):
    j = pl.program_id(1)

    @pl.when(j == 0)
    def _():
        x = x_ref[...]
        ms = jnp.mean(x * x, axis=-1, keepdims=True)
        h = (x * lax.rsqrt(ms + RMS_EPS) * n1_ref[...]).astype(BF16)
        h_scr[...] = h
        small_ref[...] = _dot(h, ws_ref[...])

    acc = _dot(h_scr[...], w_ref[...])

    def head_norm(nw_ref, scale):
        for hh in range(N_HEADS):
            a = acc[:, hh * HEAD_DIM:(hh + 1) * HEAD_DIM]
            ms = jnp.mean(a * a, axis=-1, keepdims=True)
            y = a * lax.rsqrt(ms + RMS_EPS) * nw_ref[...] * scale
            proj_ref[:, hh * HEAD_DIM:(hh + 1) * HEAD_DIM] = y.astype(BF16)

    @pl.when(j == BLK_FQ)
    def _():
        head_norm(qn_ref, HEAD_DIM ** -0.5)

    @pl.when(j == BLK_FK)
    def _():
        head_norm(kn_ref, 1.0)

    @pl.when(jnp.logical_and(j != BLK_FQ, j != BLK_FK))
    def _():
        proj_ref[...] = acc.astype(BF16)


def _in_proj(x2d, n1, w_big, w_small, qn, kn):
    T, D = x2d.shape
    n_cols = w_big.shape[1]
    tm = min(512, T)
    tn = HEADS_W
    return pl.pallas_call(
        _in_proj_kernel,
        grid=(T // tm, n_cols // tn),
        in_specs=[
            pl.BlockSpec((tm, D), lambda i, j: (i, 0)),
            pl.BlockSpec((1, D), lambda i, j: (0, 0)),
            pl.BlockSpec((D, tn), lambda i, j: (0, j)),
            pl.BlockSpec((D, LANES), lambda i, j: (0, 0)),
            pl.BlockSpec((1, HEAD_DIM), lambda i, j: (0, 0)),
            pl.BlockSpec((1, HEAD_DIM), lambda i, j: (0, 0)),
        ],
        out_specs=[
            pl.BlockSpec((tm, tn), lambda i, j: (i, j)),
            pl.BlockSpec((tm, LANES), lambda i, j: (i, 0)),
        ],
        out_shape=[
            jax.ShapeDtypeStruct((T, n_cols), BF16),
            jax.ShapeDtypeStruct((T, LANES), F32),
        ],
        scratch_shapes=[pltpu.VMEM((tm, D), BF16)],
        compiler_params=_cparams(("arbitrary", "arbitrary")),
        name="in_proj",
    )(x2d, n1, w_big, w_small, qn, kn)


LANE_C, LANE_BETA, LANE_GC = 0, N_HEADS, 2 * N_HEADS


def _gates_kernel(s_ref, p_ref, g_ref, carry, *, tiles_per_seq):
    i = pl.program_id(0)
    tg = s_ref.shape[0]

    @pl.when(i % tiles_per_seq == 0)
    def _():
        carry[...] = jnp.zeros_like(carry)

    z = s_ref[...] + p_ref[0:1, :]
    soft = jnp.log1p(jnp.exp(-jnp.abs(z)))
    log_sig = jnp.minimum(z, 0.0) - soft
    softplus = jnp.maximum(z, 0.0) + soft
    beta = jax.nn.sigmoid(z)
    g = -jnp.exp(p_ref[1:2, :]) * softplus

    row = lax.broadcasted_iota(jnp.int32, (tg, tg), 0)
    col = lax.broadcasted_iota(jnp.int32, (tg, tg), 1)
    tri = row >= col
    tri_chunk = jnp.logical_and(tri, (row >> CHUNK_SHIFT) == (col >> CHUNK_SHIFT))
    c = _dot_exact(tri.astype(F32), log_sig) + carry[...]
    gc = _dot_exact(tri_chunk.astype(F32), g)
    carry[...] = c[tg - 1:tg, :]

    lane = lax.broadcasted_iota(jnp.int32, (tg, LANES), 1)
    g_ref[...] = jnp.where(lane < LANE_BETA, c,
                           jnp.where(lane < LANE_GC, beta,
                                     jnp.where(lane < LANE_GC + N_HEADS, gc, 0.0)))


def _gates(small, params, seq_len):
    T = small.shape[0]
    tg = min(256, seq_len)
    return pl.pallas_call(
        functools.partial(_gates_kernel, tiles_per_seq=seq_len // tg),
        grid=(T // tg,),
        in_specs=[
            pl.BlockSpec((tg, LANES), lambda i: (i, 0)),
            pl.BlockSpec((8, LANES), lambda i: (0, 0)),
        ],
        out_specs=pl.BlockSpec((tg, LANES), lambda i: (i, 0)),
        out_shape=jax.ShapeDtypeStruct((T, LANES), F32),
        scratch_shapes=[pltpu.VMEM((1, LANES), F32)],
        compiler_params=_cparams(("arbitrary",)),
        name="gates",
    )(small, params)


def _fox_kernel(qi_ref, kj_ref, q_ref, k_ref, v_ref, g_ref, cr_ref, o_ref,
                m_scr, l_scr, acc_scr, cc_scr):
    h = pl.program_id(1)
    p = pl.program_id(2)
    i = qi_ref[p]
    j = kj_ref[p]
    tq, tk = q_ref.shape[0], k_ref.shape[0]

    @pl.when(j == 0)
    def _():
        m_scr[...] = jnp.full_like(m_scr, NEG_BIG)
        l_scr[...] = jnp.zeros_like(l_scr)
        acc_scr[...] = jnp.zeros_like(acc_scr)
        lane = lax.broadcasted_iota(jnp.int32, (tq, LANES), 1)
        cc = jnp.sum(jnp.where(lane == h + LANE_C, g_ref[...], 0.0), axis=1, keepdims=True)
        cc_scr[...] = jnp.broadcast_to(cc, (tq, LANES))

    def step(masked):
        s = _nt_dot(q_ref[...], k_ref[...])
        s = s + (cc_scr[:, 0:1] - cr_ref[...])
        if masked:
            row = lax.broadcasted_iota(jnp.int32, (tq, tk), 0)
            col = lax.broadcasted_iota(jnp.int32, (tq, tk), 1)
            s = jnp.where(row >= col, s, -jnp.inf)
        m_old = m_scr[:, 0:1]
        m_new = jnp.maximum(m_old, jnp.max(s, axis=1, keepdims=True))
        alpha = jnp.exp(m_old - m_new)
        pexp = jnp.exp(s - m_new)
        l_new = alpha * l_scr[:, 0:1] + jnp.sum(pexp, axis=1, keepdims=True)
        acc_scr[...] = alpha * acc_scr[...] + _dot(pexp.astype(BF16), v_ref[...])
        m_scr[...] = jnp.broadcast_to(m_new, (tq, LANES))
        l_scr[...] = jnp.broadcast_to(l_new, (tq, LANES))

    @pl.when(j < i)
    def _():
        step(False)

    @pl.when(j == i)
    def _():
        step(True)
        o_ref[...] = (acc_scr[...] / l_scr[:, 0:1]).astype(o_ref.dtype)


def _fox(proj, gates, c_rows, batch, seq_len):
    T = proj.shape[0]
    tq = min(512, seq_len)
    nq = seq_len // tq
    pairs = [(i, j) for i in range(nq) for j in range(i + 1)]
    qi = jnp.asarray([p[0] for p in pairs], jnp.int32)
    kj = jnp.asarray([p[1] for p in pairs], jnp.int32)
    q_off, k_off, v_off = BLK_FQ * N_HEADS, BLK_FK * N_HEADS, BLK_FV * N_HEADS
    grid_spec = pltpu.PrefetchScalarGridSpec(
        num_scalar_prefetch=2,
        grid=(batch, N_HEADS, len(pairs)),
        in_specs=[
            pl.BlockSpec((tq, HEAD_DIM), lambda b, h, p, qi, kj: (b * nq + qi[p], q_off + h)),
            pl.BlockSpec((tq, HEAD_DIM), lambda b, h, p, qi, kj: (b * nq + kj[p], k_off + h)),
            pl.BlockSpec((tq, HEAD_DIM), lambda b, h, p, qi, kj: (b * nq + kj[p], v_off + h)),
            pl.BlockSpec((tq, LANES), lambda b, h, p, qi, kj: (b * nq + qi[p], 0)),
            pl.BlockSpec((None, 1, tq), lambda b, h, p, qi, kj: (h, 0, b * nq + kj[p])),
        ],
        out_specs=pl.BlockSpec((tq, HEAD_DIM), lambda b, h, p, qi, kj: (b * nq + qi[p], h)),
        scratch_shapes=[
            pltpu.VMEM((tq, LANES), F32),
            pltpu.VMEM((tq, LANES), F32),
            pltpu.VMEM((tq, HEAD_DIM), F32),
            pltpu.VMEM((tq, LANES), F32),
        ],
    )
    return pl.pallas_call(
        _fox_kernel,
        grid_spec=grid_spec,
        out_shape=jax.ShapeDtypeStruct((T, HEADS_W), BF16),
        compiler_params=_cparams(("arbitrary", "arbitrary", "arbitrary")),
        name="fox",
    )(qi, kj, proj, proj, proj, gates, c_rows)


GDN_TILE = 2 * GDN_CHUNK
HALO = 16


def _gdn_prep_kernel(q_ref, k_ref, v_ref, hq_ref, hk_ref, hv_ref, cw_ref, g_ref, gt_ref,
                     u_ref, w_ref, qd_ref, at_ref, kdt_ref, gl_ref, xcat, *, tiles_per_seq):
    i = pl.program_id(0)
    R = GDN_TILE
    first = (i % tiles_per_seq) == 0

    for grp, (m_ref, h_ref) in enumerate(((q_ref, hq_ref), (k_ref, hk_ref), (v_ref, hv_ref))):
        halo = h_ref[...].astype(F32)
        xcat[grp, 0:HALO, :] = jnp.where(first, 0.0, halo)
        xcat[grp, HALO:HALO + R, :] = m_ref[...].astype(F32)

    row = lax.broadcasted_iota(jnp.int32, (R, R), 0)
    col = lax.broadcasted_iota(jnp.int32, (R, R), 1)
    same = (row >> CHUNK_SHIFT) == (col >> CHUNK_SHIFT)
    incl = jnp.logical_and(same, row >= col)
    strict = jnp.logical_and(same, row > col)
    eye = (row == col).astype(F32)
    top = row < GDN_CHUNK

    def conv_silu(grp, hh):
        cs = hh * HEAD_DIM
        acc = None
        for tap in range(CONV_WIDTH):
            start = HALO - (CONV_WIDTH - 1) + tap
            xs = xcat[grp, start:start + R, cs:cs + HEAD_DIM]
            wv = cw_ref[tap:tap + 1, grp * HEADS_W + cs:grp * HEADS_W + cs + HEAD_DIM]
            acc = xs * wv if acc is None else acc + xs * wv
        return _silu(acc)

    for hh in range(N_HEADS):
        cs = hh * HEAD_DIM
        q = conv_silu(0, hh)
        k = conv_silu(1, hh)
        v = conv_silu(2, hh)
        qn = q * lax.rsqrt(jnp.sum(q * q, axis=-1, keepdims=True) + RMS_EPS) * (HEAD_DIM ** -0.5)
        kn = k * lax.rsqrt(jnp.sum(k * k, axis=-1, keepdims=True) + RMS_EPS)
        qb, kb, vb = qn.astype(BF16), kn.astype(BF16), v.astype(BF16)

        gcol = jnp.broadcast_to(g_ref[:, LANE_GC + hh:LANE_GC + hh + 1], (R, R))
        bcol = jnp.broadcast_to(g_ref[:, LANE_BETA + hh:LANE_BETA + hh + 1], (R, R))
        grow = gt_ref[LANE_GC + hh:LANE_GC + hh + 1, :]
        brow = gt_ref[LANE_BETA + hh:LANE_BETA + hh + 1, :]

        decay = jnp.where(incl, jnp.exp(jnp.minimum(gcol - grow, 0.0)), 0.0)
        kk = _nt_dot(kb, kb)
        qk = _nt_dot(qb, kb)
        lmat = jnp.where(strict, kk * bcol * decay, 0.0)
        attn = qk * decay

        inv = eye - lmat
        power = lmat
        n_sq = GDN_CHUNK.bit_length() - 2
        for _ in range(n_sq):
            pb = power.astype(BF16)
            power = _dot(pb, pb)
            inv = inv + _dot(inv.astype(BF16), power.astype(BF16))

        egc_row = jnp.exp(grow)
        u = _dot((inv * brow).astype(BF16), vb)
        w = _dot((inv * (brow * egc_row)).astype(BF16), kb)

        g_last0 = gcol[GDN_CHUNK - 1:GDN_CHUNK, :]
        g_last1 = gcol[R - 1:R, :]
        g_last = jnp.where(top, g_last0, g_last1)
        q_dec = qn * jnp.exp(gcol)
        k_dec = kn * jnp.exp(g_last - gcol)

        attn_own = jnp.where(top, attn, pltpu.roll(attn, GDN_CHUNK, 1))
        attn_own = jnp.where(col < GDN_CHUNK, attn_own, 0.0)

        u_ref[:, cs:cs + HEAD_DIM] = u.astype(BF16)
        w_ref[:, cs:cs + HEAD_DIM] = w.astype(BF16)
        qd_ref[:, cs:cs + HEAD_DIM] = q_dec.astype(BF16)
        at_ref[:, cs:cs + HEAD_DIM] = attn_own.astype(BF16)
        kdt_ref[cs:cs + HEAD_DIM, :] = k_dec.T.astype(BF16)
        sub = lax.broadcasted_iota(jnp.int32, (8, R), 0)
        gl_ref[:, cs:cs + HEAD_DIM] = jnp.where(
            sub == 0, jnp.exp(g_last0), jnp.where(sub == 1, jnp.exp(g_last1), 0.0))


def _gdn_prep(proj, conv_w, gates, gates_t, seq_len):
    T = proj.shape[0]
    R = GDN_TILE
    n_tiles = T // R
    q_blk, k_blk, v_blk = BLK_GQ, BLK_GK, BLK_GV
    per_tile = R // HALO

    def main_spec(cb):
        return pl.BlockSpec((R, HEADS_W), lambda i: (i, cb))

    def halo_spec(cb):
        return pl.BlockSpec((HALO, HEADS_W), lambda i: (jnp.maximum(i * per_tile - 1, 0), cb))

    wide = pl.BlockSpec((R, HEADS_W), lambda i: (i, 0))
    return pl.pallas_call(
        functools.partial(_gdn_prep_kernel, tiles_per_seq=seq_len // R),
        grid=(n_tiles,),
        in_specs=[
            main_spec(q_blk), main_spec(k_blk), main_spec(v_blk),
            halo_spec(q_blk), halo_spec(k_blk), halo_spec(v_blk),
            pl.BlockSpec((CONV_WIDTH, 3 * HEADS_W), lambda i: (0, 0)),
            pl.BlockSpec((R, LANES), lambda i: (i, 0)),
            pl.BlockSpec((32, R), lambda i: (0, i)),
        ],
        out_specs=[
            wide, wide, wide, wide,
            pl.BlockSpec((HEADS_W, R), lambda i: (0, i)),
            pl.BlockSpec((8, HEADS_W), lambda i: (i, 0)),
        ],
        out_shape=[
            jax.ShapeDtypeStruct((T, HEADS_W), BF16),
            jax.ShapeDtypeStruct((T, HEADS_W), BF16),
            jax.ShapeDtypeStruct((T, HEADS_W), BF16),
            jax.ShapeDtypeStruct((T, HEADS_W), BF16),
            jax.ShapeDtypeStruct((HEADS_W, T), BF16),
            jax.ShapeDtypeStruct((n_tiles * 8, HEADS_W), F32),
        ],
        scratch_shapes=[pltpu.VMEM((3, HALO + R, HEADS_W), F32)],
        compiler_params=_cparams(("arbitrary",)),
        name="gdn_prep",
    )(proj, proj, proj, proj, proj, proj, conv_w, gates, gates_t)


def _gdn_scan_kernel(u_ref, w_ref, qd_ref, at_ref, kdt_ref, gl_ref, z_ref, nw_ref, y_ref, state):
    s = pl.program_id(1)
    rows = u_ref.shape[0]

    @pl.when(s == 0)
    def _():
        state[...] = jnp.zeros_like(state)

    zeros_half = jnp.zeros((GDN_CHUNK, HEAD_DIM), BF16)
    for c in range(rows // GDN_CHUNK):
        r0 = c * GDN_CHUNK
        tile, half = divmod(c, 2)
        t0 = tile * GDN_TILE
        for hh in range(N_HEADS):
            cs = hh * HEAD_DIM
            st = state[hh]
            sb = st.astype(BF16)
            ws = _dot(w_ref[r0:r0 + GDN_CHUNK, cs:cs + HEAD_DIM], sb)
            qs = _dot(qd_ref[r0:r0 + GDN_CHUNK, cs:cs + HEAD_DIM], sb)
            v_new = (u_ref[r0:r0 + GDN_CHUNK, cs:cs + HEAD_DIM].astype(F32) - ws).astype(BF16)
            v_lo = jnp.concatenate([v_new, zeros_half], axis=0)
            v_own = v_lo if half == 0 else jnp.concatenate([zeros_half, v_new], axis=0)
            o = qs + _dot(at_ref[r0:r0 + GDN_CHUNK, cs:cs + HEAD_DIM], v_lo)
            gl = gl_ref[tile * 8 + half:tile * 8 + half + 1, cs:cs + HEAD_DIM]
            state[hh] = st * gl + _dot(kdt_ref[cs:cs + HEAD_DIM, t0:t0 + GDN_TILE], v_own)

            ms = jnp.mean(o * o, axis=-1, keepdims=True)
            z = z_ref[r0:r0 + GDN_CHUNK, cs:cs + HEAD_DIM].astype(F32)
            y = o * lax.rsqrt(ms + RMS_EPS) * nw_ref[...] * _silu(z)
            y_ref[r0:r0 + GDN_CHUNK, cs:cs + HEAD_DIM] = y.astype(y_ref.dtype)


def _gdn_scan(u, w, qd, at, kdt, gl, proj, out_norm_w, batch, seq_len):
    T = u.shape[0]
    rows = min(256, seq_len)
    ns = seq_len // rows
    z_blk = BLK_GZ
    wide = pl.BlockSpec((rows, HEADS_W), lambda b, s: (b * ns + s, 0))
    gl_rows = rows // GDN_TILE * 8
    return pl.pallas_call(
        _gdn_scan_kernel,
        grid=(batch, ns),
        in_specs=[
            wide, wide, wide, wide,
            pl.BlockSpec((HEADS_W, rows), lambda b, s: (0, b * ns + s)),
            pl.BlockSpec((gl_rows, HEADS_W), lambda b, s: (b * ns + s, 0)),
            pl.BlockSpec((rows, HEADS_W), lambda b, s: (b * ns + s, z_blk)),
            pl.BlockSpec((1, HEAD_DIM), lambda b, s: (0, 0)),
        ],
        out_specs=wide,
        out_shape=jax.ShapeDtypeStruct((T, HEADS_W), BF16),
        scratch_shapes=[pltpu.VMEM((N_HEADS, HEAD_DIM, HEAD_DIM), F32)],
        compiler_params=_cparams(("arbitrary", "arbitrary")),
        name="gdn_scan",
    )(u, w, qd, at, kdt, gl, proj, out_norm_w)


def _merge_kernel(yf_ref, yg_ref, ga_ref, gb_ref, x_ref, wf_ref, wg_ref, wo_ref, n2_ref,
                  wr_ref, br_ref, x2_ref, h2_ref, lg_ref):
    a = _dot(yf_ref[...], wf_ref[...])
    b = _dot(yg_ref[...], wg_ref[...])
    merged = (jax.nn.sigmoid(ga_ref[...].astype(F32)) * a
              + jax.nn.sigmoid(gb_ref[...].astype(F32)) * b)
    x2 = x_ref[...] + _dot(merged.astype(BF16), wo_ref[...])
    x2_ref[...] = x2
    ms = jnp.mean(x2 * x2, axis=-1, keepdims=True)
    h2 = x2 * lax.rsqrt(ms + RMS_EPS) * n2_ref[...]
    h2_ref[...] = h2
    lg_ref[...] = _dot_exact(h2, wr_ref[...]) + br_ref[...]


def _merge(y_fox, y_gdn, proj, x2d, wf, wg, wo, n2, wr, br):
    T, D = x2d.shape
    tm = min(256, T)
    ga_blk, gb_blk = BLK_GATE_A * HEADS_W // D, BLK_GATE_B * HEADS_W // D
    const = lambda shape: pl.BlockSpec(shape, lambda i: (0, 0), pipeline_mode=pl.Buffered(1))
    return pl.pallas_call(
        _merge_kernel,
        grid=(T // tm,),
        in_specs=[
            pl.BlockSpec((tm, HEADS_W), lambda i: (i, 0)),
            pl.BlockSpec((tm, HEADS_W), lambda i: (i, 0)),
            pl.BlockSpec((tm, D), lambda i: (i, ga_blk)),
            pl.BlockSpec((tm, D), lambda i: (i, gb_blk)),
            pl.BlockSpec((tm, D), lambda i: (i, 0)),
            const((HEADS_W, D)), const((HEADS_W, D)), const((D, D)), const((1, D)),
            const((D, LANES)), const((1, LANES)),
        ],
        out_specs=[
            pl.BlockSpec((tm, D), lambda i: (i, 0)),
            pl.BlockSpec((tm, D), lambda i: (i, 0)),
            pl.BlockSpec((tm, LANES), lambda i: (i, 0)),
        ],
        out_shape=[
            jax.ShapeDtypeStruct((T, D), F32),
            jax.ShapeDtypeStruct((T, D), F32),
            jax.ShapeDtypeStruct((T, LANES), F32),
        ],
        compiler_params=_cparams(("arbitrary",)),
        name="merge",
    )(y_fox, y_gdn, proj, proj, x2d, wf, wg, wo, n2, wr, br)


LANE_RANK = TOP_K


def _route_kernel(lg_ref, ri_ref, rw_ref, cnt_ref, carry):
    i = pl.program_id(0)
    tm = lg_ref.shape[0]

    @pl.when(i == 0)
    def _():
        carry[...] = jnp.zeros_like(carry)

    lane = lax.broadcasted_iota(jnp.int32, (tm, LANES), 1)
    lane_f = lane.astype(F32)
    v = jnp.where(lane < N_EXPERTS, lg_ref[...], -jnp.inf)
    vals, idxs, hots = [], [], []
    for _ in range(TOP_K):
        m = jnp.max(v, axis=1, keepdims=True)
        idx = jnp.min(jnp.where(v == m, lane_f, float(LANES)), axis=1, keepdims=True)
        hot = lane_f == idx
        vals.append(m)
        idxs.append(idx)
        hots.append(hot)
        v = jnp.where(hot, -jnp.inf, v)

    exps = [jnp.exp(val - vals[0]) for val in vals]
    den = exps[0] + exps[1] + exps[2] + exps[3]

    multi_hot = jnp.zeros((tm, LANES), F32)
    for hot in hots:
        multi_hot = multi_hot + hot.astype(F32)
    row = lax.broadcasted_iota(jnp.int32, (tm, tm), 0)
    col = lax.broadcasted_iota(jnp.int32, (tm, tm), 1)
    before = _dot((row > col).astype(BF16), multi_hot.astype(BF16)) + carry[0:1, :]
    total = before[tm - 1:tm, :] + multi_hot[tm - 1:tm, :]
    carry[...] = jnp.broadcast_to(total, carry.shape)
    cnt_ref[...] = jnp.broadcast_to(total, cnt_ref.shape)

    out_i = jnp.zeros((tm, LANES), jnp.int32)
    out_w = jnp.zeros((tm, LANES), F32)
    for k in range(TOP_K):
        rank = jnp.sum(jnp.where(hots[k], before, 0.0), axis=1, keepdims=True).astype(jnp.int32)
        out_i = jnp.where(lane == k, idxs[k].astype(jnp.int32), out_i)
        out_i = jnp.where(lane == LANE_RANK + k, rank, out_i)
        out_w = jnp.where(lane == k, exps[k] / den, out_w)
    ri_ref[...] = out_i
    rw_ref[...] = out_w


def _route(logits):
    T = logits.shape[0]
    tm = min(256, T)
    return pl.pallas_call(
        _route_kernel,
        grid=(T // tm,),
        in_specs=[pl.BlockSpec((tm, LANES), lambda i: (i, 0))],
        out_specs=[
            pl.BlockSpec((tm, LANES), lambda i: (i, 0)),
            pl.BlockSpec((tm, LANES), lambda i: (i, 0)),
            pl.BlockSpec((8, LANES), lambda i: (0, 0)),
        ],
        out_shape=[
            jax.ShapeDtypeStruct((T, LANES), jnp.int32),
            jax.ShapeDtypeStruct((T, LANES), F32),
            jax.ShapeDtypeStruct((8, LANES), F32),
        ],
        scratch_shapes=[pltpu.VMEM((8, LANES), F32)],
        compiler_params=_cparams(("arbitrary",)),
        name="route",
    )(logits)


DISPATCH_TOKENS = 256


def _dispatch_kernel(dest_ref, h_ref, xs_in_ref, xs_ref, sem):
    del xs_in_ref
    i = pl.program_id(0)
    n = DISPATCH_TOKENS * TOP_K

    def row_copy(a):
        t = i * DISPATCH_TOKENS + a // TOP_K
        return pltpu.make_async_copy(h_ref.at[pl.ds(t, 1)], xs_ref.at[pl.ds(dest_ref[a], 1)], sem)

    def start(a, carry):
        row_copy(a).start()
        return carry

    def wait(a, carry):
        row_copy(a).wait()
        return carry

    lax.fori_loop(0, n, start, 0)
    lax.fori_loop(0, n, wait, 0)


def _dispatch(dest_flat, h2, xs_zero):
    T = h2.shape[0]
    n = DISPATCH_TOKENS * TOP_K
    return pl.pallas_call(
        _dispatch_kernel,
        grid=(T // DISPATCH_TOKENS,),
        in_specs=[
            pl.BlockSpec((n,), lambda i: (i,), memory_space=pltpu.SMEM),
            pl.BlockSpec(memory_space=pl.ANY),
            pl.BlockSpec(memory_space=pl.ANY),
        ],
        out_specs=pl.BlockSpec(memory_space=pl.ANY),
        out_shape=jax.ShapeDtypeStruct(xs_zero.shape, xs_zero.dtype),
        scratch_shapes=[pltpu.SemaphoreType.DMA(())],
        input_output_aliases={2: 0},
        compiler_params=pltpu.CompilerParams(
            dimension_semantics=("arbitrary",), has_side_effects=True),
        name="dispatch",
    )(dest_flat, h2, xs_zero)


def _moe_up_kernel(be_ref, nu_ref, x_ref, wg_ref, wl_ref, bg_ref, bl_ref, act_ref):
    b = pl.program_id(1)

    @pl.when(b < nu_ref[0])
    def _():
        x = x_ref[...].astype(BF16)
        gate = jnp.minimum(_dot(x, wg_ref[...]) + bg_ref[...], SWIGLU_LIMIT)
        lin = jnp.clip(_dot(x, wl_ref[...]) + bl_ref[...], -SWIGLU_LIMIT, SWIGLU_LIMIT)
        act = gate * jax.nn.sigmoid(SWIGLU_ALPHA * gate) * (lin + 1.0)
        act_ref[...] = act.astype(act_ref.dtype)

    @pl.when(b >= nu_ref[0])
    def _():
        act_ref[...] = jnp.zeros_like(act_ref)


def _moe_up(block_expert, n_used, xs, w_gate, w_lin, b_gate, b_lin):
    P, D = xs.shape
    dff = w_gate.shape[2]
    tn = 512
    nb = P // MOE_ROWS

    def blk(b, nu):
        return jnp.minimum(b, nu[0] - 1)

    grid_spec = pltpu.PrefetchScalarGridSpec(
        num_scalar_prefetch=2,
        grid=(dff // tn, nb),
        in_specs=[
            pl.BlockSpec((MOE_ROWS, D), lambda j, b, be, nu: (blk(b, nu), 0)),
            pl.BlockSpec((None, D, tn), lambda j, b, be, nu: (be[blk(b, nu)], 0, j)),
            pl.BlockSpec((None, D, tn), lambda j, b, be, nu: (be[blk(b, nu)], 0, j)),
            pl.BlockSpec((None, 1, tn), lambda j, b, be, nu: (be[blk(b, nu)], 0, j)),
            pl.BlockSpec((None, 1, tn), lambda j, b, be, nu: (be[blk(b, nu)], 0, j)),
        ],
        out_specs=pl.BlockSpec((MOE_ROWS, tn), lambda j, b, be, nu: (b, j)),
    )
    return pl.pallas_call(
        _moe_up_kernel,
        grid_spec=grid_spec,
        out_shape=jax.ShapeDtypeStruct((P, dff), BF16),
        compiler_params=_cparams(("arbitrary", "arbitrary")),
        name="moe_up",
    )(block_expert, n_used, xs, w_gate, w_lin, b_gate, b_lin)


def _moe_down_kernel(be_ref, nu_ref, a_ref, wd_ref, bd_ref, y_ref):
    b = pl.program_id(1)

    @pl.when(b < nu_ref[0])
    def _():
        y_ref[...] = _dot(a_ref[...], wd_ref[...]) + bd_ref[...]

    @pl.when(b >= nu_ref[0])
    def _():
        y_ref[...] = jnp.zeros_like(y_ref)


def _moe_down(block_expert, n_used, act, w_down, b_down):
    P, dff = act.shape
    D = w_down.shape[2]
    tn = 1024
    nb = P // MOE_ROWS

    def blk(b, nu):
        return jnp.minimum(b, nu[0] - 1)

    grid_spec = pltpu.PrefetchScalarGridSpec(
        num_scalar_prefetch=2,
        grid=(D // tn, nb),
        in_specs=[
            pl.BlockSpec((MOE_ROWS, dff), lambda j, b, be, nu: (blk(b, nu), 0)),
            pl.BlockSpec((None, dff, tn), lambda j, b, be, nu: (be[blk(b, nu)], 0, j)),
            pl.BlockSpec((None, 1, tn), lambda j, b, be, nu: (be[blk(b, nu)], 0, j)),
        ],
        out_specs=pl.BlockSpec((MOE_ROWS, tn), lambda j, b, be, nu: (b, j)),
    )
    return pl.pallas_call(
        _moe_down_kernel,
        grid_spec=grid_spec,
        out_shape=jax.ShapeDtypeStruct((P, D), F32),
        compiler_params=_cparams(("arbitrary", "arbitrary")),
        name="moe_down",
    )(block_expert, n_used, act, w_down, b_down)


COMBINE_TOKENS = 256


def _combine_kernel(dest_ref, y_ref, x2_ref, rw_ref, o_ref, buf, sem):
    n = COMBINE_TOKENS * TOP_K

    def row_copy(a):
        return pltpu.make_async_copy(
            y_ref.at[pl.ds(dest_ref[a], 1)], buf.at[a % TOP_K, pl.ds(a // TOP_K, 1)], sem)

    def start(a, carry):
        row_copy(a).start()
        return carry

    def wait(a, carry):
        row_copy(a).wait()
        return carry

    lax.fori_loop(0, n, start, 0)
    lax.fori_loop(0, n, wait, 0)
    acc = x2_ref[...]
    for k in range(TOP_K):
        acc = acc + rw_ref[:, k:k + 1] * buf[k]
    o_ref[...] = acc


def _combine(dest_flat, y, x2, route_w):
    T, D = x2.shape
    tc = COMBINE_TOKENS
    return pl.pallas_call(
        _combine_kernel,
        grid=(T // tc,),
        in_specs=[
            pl.BlockSpec((tc * TOP_K,), lambda i: (i,), memory_space=pltpu.SMEM),
            pl.BlockSpec(memory_space=pl.ANY),
            pl.BlockSpec((tc, D), lambda i: (i, 0)),
            pl.BlockSpec((tc, LANES), lambda i: (i, 0)),
        ],
        out_specs=pl.BlockSpec((tc, D), lambda i: (i, 0)),
        out_shape=jax.ShapeDtypeStruct((T, D), F32),
        scratch_shapes=[pltpu.VMEM((TOP_K, tc, D), F32), pltpu.SemaphoreType.DMA(())],
        compiler_params=_cparams(("arbitrary",)),
        name="combine",
    )(dest_flat, y, x2, route_w)


def _pad_lanes(v, offset):
    return jnp.zeros((LANES,), F32).at[offset:offset + v.shape[0]].set(v.astype(F32))


def _layer(x, norm1_w, w_in, b_fox_f, fox_q_norm_w, fox_k_norm_w, gdn_conv_w, gdn_A_log,
           gdn_dt_bias, gdn_out_norm_w, w_up_fox, w_up_gdn, w_o, norm2_w, w_router, b_router,
           w_gate_up, b_gate_up, w_down, b_down):
    B, S, D = x.shape
    T = B * S
    x2d = x.reshape(T, D)

    o_f = 3 * HEADS_W
    o_gqkv = o_f + N_HEADS
    o_gb = o_gqkv + 4 * HEADS_W
    o_gate = o_gb + 2 * N_HEADS
    w_big = jnp.concatenate(
        [w_in[:, o_gate:], w_in[:, :o_f], w_in[:, o_gqkv:o_gb]], axis=1).astype(BF16)
    w_small = jnp.concatenate(
        [w_in[:, o_f:o_gqkv], w_in[:, o_gb:o_gate],
         jnp.zeros((D, LANES - 3 * N_HEADS), w_in.dtype)], axis=1).astype(BF16)

    proj, small = _in_proj(x2d, norm1_w.reshape(1, D), w_big, w_small,
                           fox_q_norm_w.reshape(1, HEAD_DIM), fox_k_norm_w.reshape(1, HEAD_DIM))

    gate_params = jnp.zeros((8, LANES), F32)
    gate_params = gate_params.at[0].set(_pad_lanes(b_fox_f, LANE_C) + _pad_lanes(gdn_dt_bias, LANE_GC))
    gate_params = gate_params.at[1].set(_pad_lanes(gdn_A_log, LANE_GC))
    gates = _gates(small, gate_params, S)
    gates_t = gates[:, :32].T
    c_rows = gates_t[LANE_C:LANE_C + N_HEADS].reshape(N_HEADS, 1, T)

    y_fox = _fox(proj, gates, c_rows, B, S)

    u, w, qd, at, kdt, gl = _gdn_prep(proj, gdn_conv_w.astype(F32), gates, gates_t, S)
    y_gdn = _gdn_scan(u, w, qd, at, kdt, gl, proj, gdn_out_norm_w.reshape(1, HEAD_DIM), B, S)

    w_router_p = jnp.zeros((D, LANES), F32).at[:, :N_EXPERTS].set(w_router.astype(F32))
    b_router_p = _pad_lanes(b_router, 0).reshape(1, LANES)
    x2, h2, logits = _merge(y_fox, y_gdn, proj, x2d, w_up_fox.astype(BF16), w_up_gdn.astype(BF16),
                            w_o.astype(BF16), norm2_w.reshape(1, D), w_router_p, b_router_p)

    route_i, route_w, counts = _route(logits)
    expert = route_i[:, :TOP_K]
    rank = route_i[:, LANE_RANK:LANE_RANK + TOP_K]
    counts = counts[0, :N_EXPERTS].astype(jnp.int32)
    padded = (counts + MOE_ROWS - 1) // MOE_ROWS * MOE_ROWS
    pad_end = jnp.cumsum(padded)
    pad_start = pad_end - padded
    dest_flat = (pad_start[expert] + rank).reshape(T * TOP_K).astype(jnp.int32)
    n_blocks = -(-T * TOP_K // MOE_ROWS) + N_EXPERTS
    n_used = (pad_end[-1:] // MOE_ROWS).astype(jnp.int32)
    block_expert = jnp.minimum(
        jnp.searchsorted(pad_end, jnp.arange(n_blocks, dtype=jnp.int32) * MOE_ROWS, side="right"),
        N_EXPERTS - 1).astype(jnp.int32)

    xs = _dispatch(dest_flat, h2, jnp.zeros((n_blocks * MOE_ROWS, D), F32))

    dff = w_down.shape[1]
    w_gu = w_gate_up.reshape(N_EXPERTS, D, dff, 2)
    b_gu = b_gate_up.reshape(N_EXPERTS, 1, dff, 2)
    act = _moe_up(block_expert, n_used, xs, w_gu[..., 0].astype(BF16), w_gu[..., 1].astype(BF16),
                  b_gu[..., 0].astype(F32), b_gu[..., 1].astype(F32))
    y = _moe_down(block_expert, n_used, act, w_down.astype(BF16),
                  b_down.reshape(N_EXPERTS, 1, D).astype(F32))

    out = _combine(dest_flat, y, x2, route_w)
    return out.reshape(B, S, D)


def kernel(x, norm1_w, w_in, b_fox_f, fox_q_norm_w, fox_k_norm_w, gdn_conv_w, gdn_A_log, gdn_dt_bias, gdn_out_norm_w, w_up_fox, w_up_gdn, w_o, norm2_w, w_router, b_router, w_gate_up, b_gate_up, w_down, b_down):
    depth = norm1_w.shape[0]
    for l in range(depth):
        x = _layer(x, norm1_w[l], w_in[l], b_fox_f[l], fox_q_norm_w[l], fox_k_norm_w[l],
                   gdn_conv_w[l], gdn_A_log[l], gdn_dt_bias[l], gdn_out_norm_w[l], w_up_fox[l],
                   w_up_gdn[l], w_o[l], norm2_w[l], w_router[l], b_router[l], w_gate_up[l],
                   b_gate_up[l], w_down[l], b_down[l])
    return x
```

```python
import functools

import jax
import jax.numpy as jnp
from jax import lax
from jax.experimental import pallas as pl
from jax.experimental.pallas import tpu as pltpu

F32 = jnp.float32
BF16 = jnp.bfloat16

N_HEADS = 8
HEAD_DIM = 128
HEADS_W = N_HEADS * HEAD_DIM
GDN_CHUNK = 64
CONV_WIDTH = 4
N_EXPERTS = 32
TOP_K = 4
SWIGLU_LIMIT = 7.0
SWIGLU_ALPHA = 1.702
RMS_EPS = 1e-6
LANES = 128
NEG_BIG = -1e30
LOG2E = 1.4426950408889634

VMEM_LIMIT = 56 * 1024 * 1024

MOE_ROWS = 512


def _cparams(sem):
    return pltpu.CompilerParams(dimension_semantics=sem, vmem_limit_bytes=VMEM_LIMIT)


def _nt_dot(a, b):
    return lax.dot_general(a, b, (((1,), (1,)), ((), ())), preferred_element_type=F32)


def _dot(a, b):
    return jnp.dot(a, b, preferred_element_type=F32)


def _dot_exact(a, b):
    return jnp.dot(a, b, preferred_element_type=F32, precision=lax.Precision.HIGHEST)


def _silu(x):
    return x * jax.nn.sigmoid(x)


BLK_GATE_A, BLK_GATE_B = 0, 2
BLK_FQ, BLK_FK, BLK_FV = 4, 5, 6
BLK_GQ, BLK_GK, BLK_GV, BLK_GZ = 7, 8, 9, 10
CHUNK_SHIFT = GDN_CHUNK.bit_length() - 1


def _in_proj_kernel(x_ref, n1_ref, w_ref, ws_ref, qn_ref, kn_ref, proj_ref, small_ref, vt_ref,
                    h_scr):
    j = pl.program_id(1)

    @pl.when(j == 0)
    def _():
        x = x_ref[...]
        ms = jnp.mean(x * x, axis=-1, keepdims=True)
        h = (x * lax.rsqrt(ms + RMS_EPS) * n1_ref[...]).astype(BF16)
        h_scr[...] = h
        small_ref[...] = _dot(h, ws_ref[...])

    acc = _dot(h_scr[...], w_ref[...])

    def head_norm(nw_ref, scale):
        for hh in range(N_HEADS):
            a = acc[:, hh * HEAD_DIM:(hh + 1) * HEAD_DIM]
            ms = jnp.mean(a * a, axis=-1, keepdims=True)
            y = a * lax.rsqrt(ms + RMS_EPS) * nw_ref[...] * scale
            proj_ref[:, hh * HEAD_DIM:(hh + 1) * HEAD_DIM] = y.astype(BF16)

    @pl.when(j == BLK_FQ)
    def _():
        head_norm(qn_ref, HEAD_DIM ** -0.5 * LOG2E)

    @pl.when(j == BLK_FK)
    def _():
        head_norm(kn_ref, 1.0)

    @pl.when(j == BLK_FV)
    def _():
        vt_ref[...] = acc.T.astype(BF16)

    @pl.when(jnp.logical_and(j != BLK_FQ, j != BLK_FK))
    def _():
        proj_ref[...] = acc.astype(BF16)


def _in_proj(x2d, n1, w_big, w_small, qn, kn):
    T, D = x2d.shape
    n_cols = w_big.shape[1]
    tm = min(512, T)
    tn = HEADS_W
    return pl.pallas_call(
        _in_proj_kernel,
        grid=(T // tm, n_cols // tn),
        in_specs=[
            pl.BlockSpec((tm, D), lambda i, j: (i, 0)),
            pl.BlockSpec((1, D), lambda i, j: (0, 0)),
            pl.BlockSpec((D, tn), lambda i, j: (0, j)),
            pl.BlockSpec((D, LANES), lambda i, j: (0, 0)),
            pl.BlockSpec((1, HEAD_DIM), lambda i, j: (0, 0)),
            pl.BlockSpec((1, HEAD_DIM), lambda i, j: (0, 0)),
        ],
        out_specs=[
            pl.BlockSpec((tm, tn), lambda i, j: (i, j)),
            pl.BlockSpec((tm, LANES), lambda i, j: (i, 0)),
            pl.BlockSpec((HEADS_W, tm), lambda i, j: (0, i)),
        ],
        out_shape=[
            jax.ShapeDtypeStruct((T, n_cols), BF16),
            jax.ShapeDtypeStruct((T, LANES), F32),
            jax.ShapeDtypeStruct((HEADS_W, T), BF16),
        ],
        scratch_shapes=[pltpu.VMEM((tm, D), BF16)],
        compiler_params=_cparams(("arbitrary", "arbitrary")),
        name="in_proj",
    )(x2d, n1, w_big, w_small, qn, kn)


LANE_C, LANE_BETA, LANE_GC = 0, N_HEADS, 2 * N_HEADS


def _gates_kernel(s_ref, p_ref, g_ref, carry, *, tiles_per_seq):
    i = pl.program_id(0)
    tg = s_ref.shape[0]

    @pl.when(i % tiles_per_seq == 0)
    def _():
        carry[...] = jnp.zeros_like(carry)

    z = s_ref[...] + p_ref[0:1, :]
    soft = jnp.log1p(jnp.exp(-jnp.abs(z)))
    log_sig = jnp.minimum(z, 0.0) - soft
    softplus = jnp.maximum(z, 0.0) + soft
    beta = jax.nn.sigmoid(z)
    g = -jnp.exp(p_ref[1:2, :]) * softplus

    row = lax.broadcasted_iota(jnp.int32, (tg, tg), 0)
    col = lax.broadcasted_iota(jnp.int32, (tg, tg), 1)
    tri = row >= col
    tri_chunk = jnp.logical_and(tri, (row >> CHUNK_SHIFT) == (col >> CHUNK_SHIFT))
    c = _dot_exact(tri.astype(F32), log_sig) + carry[...]
    gc = _dot_exact(tri_chunk.astype(F32), g)
    carry[...] = c[tg - 1:tg, :]

    lane = lax.broadcasted_iota(jnp.int32, (tg, LANES), 1)
    g_ref[...] = jnp.where(lane < LANE_BETA, c,
                           jnp.where(lane < LANE_GC, beta,
                                     jnp.where(lane < LANE_GC + N_HEADS, gc, 0.0)))


def _gates(small, params, seq_len):
    T = small.shape[0]
    tg = min(256, seq_len)
    return pl.pallas_call(
        functools.partial(_gates_kernel, tiles_per_seq=seq_len // tg),
        grid=(T // tg,),
        in_specs=[
            pl.BlockSpec((tg, LANES), lambda i: (i, 0)),
            pl.BlockSpec((8, LANES), lambda i: (0, 0)),
        ],
        out_specs=pl.BlockSpec((tg, LANES), lambda i: (i, 0)),
        out_shape=jax.ShapeDtypeStruct((T, LANES), F32),
        scratch_shapes=[pltpu.VMEM((1, LANES), F32)],
        compiler_params=_cparams(("arbitrary",)),
        name="gates",
    )(small, params)


def _fox_kernel(qi_ref, kj_ref, q_ref, k_ref, vt_ref, cb_ref, o_ref, m_scr, l_scr, acc_scr):
    p = pl.program_id(2)
    i = qi_ref[p]
    j = kj_ref[p]
    tq, tk = q_ref.shape[0], k_ref.shape[0]

    @pl.when(j == 0)
    def _():
        m_scr[...] = jnp.full_like(m_scr, NEG_BIG)
        l_scr[...] = jnp.zeros_like(l_scr)
        acc_scr[...] = jnp.zeros_like(acc_scr)

    def step(masked):
        st = _nt_dot(k_ref[...], q_ref[...])
        st = st - jnp.concatenate([cb_ref[...]] * (tq // LANES), axis=1)
        if masked:
            key = lax.broadcasted_iota(jnp.int32, (tk, tq), 0)
            qry = lax.broadcasted_iota(jnp.int32, (tk, tq), 1)
            st = jnp.where(qry >= key, st, -jnp.inf)
        m_old = m_scr[...]
        m_new = jnp.maximum(m_old, jnp.max(st, axis=0, keepdims=True))
        alpha = jnp.exp2(m_old - m_new)
        pt = jnp.exp2(st - m_new)
        l_scr[...] = alpha * l_scr[...] + jnp.sum(pt, axis=0, keepdims=True)
        acc_scr[...] = alpha * acc_scr[...] + _dot(vt_ref[...], pt.astype(BF16))
        m_scr[...] = m_new

    @pl.when(j < i)
    def _():
        step(False)

    @pl.when(j == i)
    def _():
        step(True)
        o_ref[...] = (acc_scr[...] / l_scr[...]).T.astype(o_ref.dtype)


def _fox(proj, v_t, c_bcast, batch, seq_len):
    T = proj.shape[0]
    tq = min(512, seq_len)
    nq = seq_len // tq
    pairs = [(i, j) for i in range(nq) for j in range(i + 1)]
    qi = jnp.asarray([p[0] for p in pairs], jnp.int32)
    kj = jnp.asarray([p[1] for p in pairs], jnp.int32)
    q_off, k_off = BLK_FQ * N_HEADS, BLK_FK * N_HEADS
    grid_spec = pltpu.PrefetchScalarGridSpec(
        num_scalar_prefetch=2,
        grid=(batch, N_HEADS, len(pairs)),
        in_specs=[
            pl.BlockSpec((tq, HEAD_DIM), lambda b, h, p, qi, kj: (b * nq + qi[p], q_off + h)),
            pl.BlockSpec((tq, HEAD_DIM), lambda b, h, p, qi, kj: (b * nq + kj[p], k_off + h)),
            pl.BlockSpec((HEAD_DIM, tq), lambda b, h, p, qi, kj: (h, b * nq + kj[p])),
            pl.BlockSpec((None, tq, LANES), lambda b, h, p, qi, kj: (h, b * nq + kj[p], 0)),
        ],
        out_specs=pl.BlockSpec((tq, HEAD_DIM), lambda b, h, p, qi, kj: (b * nq + qi[p], h)),
        scratch_shapes=[
            pltpu.VMEM((1, tq), F32),
            pltpu.VMEM((1, tq), F32),
            pltpu.VMEM((HEAD_DIM, tq), F32),
        ],
    )
    return pl.pallas_call(
        _fox_kernel,
        grid_spec=grid_spec,
        out_shape=jax.ShapeDtypeStruct((T, HEADS_W), BF16),
        compiler_params=_cparams(("arbitrary", "arbitrary", "arbitrary")),
        name="fox",
    )(qi, kj, proj, proj, v_t, c_bcast)


GDN_TILE = 2 * GDN_CHUNK
HALO = 16


def _gdn_prep_kernel(q_ref, k_ref, v_ref, hq_ref, hk_ref, hv_ref, cw_ref, g_ref, gt_ref,
                     u_ref, w_ref, qd_ref, at_ref, kdt_ref, gl_ref, xcat, *, tiles_per_seq):
    i = pl.program_id(0)
    R = GDN_TILE
    first = (i % tiles_per_seq) == 0

    for grp, (m_ref, h_ref) in enumerate(((q_ref, hq_ref), (k_ref, hk_ref), (v_ref, hv_ref))):
        halo = h_ref[...].astype(F32)
        xcat[grp, 0:HALO, :] = jnp.where(first, 0.0, halo)
        xcat[grp, HALO:HALO + R, :] = m_ref[...].astype(F32)

    row = lax.broadcasted_iota(jnp.int32, (R, R), 0)
    col = lax.broadcasted_iota(jnp.int32, (R, R), 1)
    same = (row >> CHUNK_SHIFT) == (col >> CHUNK_SHIFT)
    incl = jnp.logical_and(same, row >= col)
    strict = jnp.logical_and(same, row > col)
    eye = (row == col).astype(F32)
    top = row < GDN_CHUNK

    def conv_silu(grp, hh):
        cs = hh * HEAD_DIM
        acc = None
        for tap in range(CONV_WIDTH):
            start = HALO - (CONV_WIDTH - 1) + tap
            xs = xcat[grp, start:start + R, cs:cs + HEAD_DIM]
            wv = cw_ref[tap:tap + 1, grp * HEADS_W + cs:grp * HEADS_W + cs + HEAD_DIM]
            acc = xs * wv if acc is None else acc + xs * wv
        return _silu(acc)

    for hh in range(N_HEADS):
        cs = hh * HEAD_DIM
        q = conv_silu(0, hh)
        k = conv_silu(1, hh)
        v = conv_silu(2, hh)
        qn = q * lax.rsqrt(jnp.sum(q * q, axis=-1, keepdims=True) + RMS_EPS) * (HEAD_DIM ** -0.5)
        kn = k * lax.rsqrt(jnp.sum(k * k, axis=-1, keepdims=True) + RMS_EPS)
        qb, kb, vb = qn.astype(BF16), kn.astype(BF16), v.astype(BF16)

        gcol = jnp.broadcast_to(g_ref[:, LANE_GC + hh:LANE_GC + hh + 1], (R, R))
        bcol = jnp.broadcast_to(g_ref[:, LANE_BETA + hh:LANE_BETA + hh + 1], (R, R))
        grow = gt_ref[LANE_GC + hh:LANE_GC + hh + 1, :]
        brow = gt_ref[LANE_BETA + hh:LANE_BETA + hh + 1, :]

        decay = jnp.where(incl, jnp.exp(jnp.minimum(gcol - grow, 0.0)), 0.0)
        kk = _nt_dot(kb, kb)
        qk = _nt_dot(qb, kb)
        lmat = jnp.where(strict, kk * bcol * decay, 0.0)
        attn = qk * decay

        inv = eye - lmat
        power = lmat
        n_sq = GDN_CHUNK.bit_length() - 2
        for _ in range(n_sq):
            pb = power.astype(BF16)
            power = _dot(pb, pb)
            inv = inv + _dot(inv.astype(BF16), power.astype(BF16))

        egc_row = jnp.exp(grow)
        u = _dot((inv * brow).astype(BF16), vb)
        w = _dot((inv * (brow * egc_row)).astype(BF16), kb)

        g_last0 = gcol[GDN_CHUNK - 1:GDN_CHUNK, :]
        g_last1 = gcol[R - 1:R, :]
        g_last = jnp.where(top, g_last0, g_last1)
        q_dec = qn * jnp.exp(gcol)
        k_dec = kn * jnp.exp(g_last - gcol)

        attn_own = jnp.where(top, attn, pltpu.roll(attn, GDN_CHUNK, 1))
        attn_own = jnp.where(col < GDN_CHUNK, attn_own, 0.0)

        u_ref[:, cs:cs + HEAD_DIM] = u.astype(BF16)
        w_ref[:, cs:cs + HEAD_DIM] = w.astype(BF16)
        qd_ref[:, cs:cs + HEAD_DIM] = q_dec.astype(BF16)
        at_ref[:, cs:cs + HEAD_DIM] = attn_own.astype(BF16)
        kdt_ref[cs:cs + HEAD_DIM, :] = k_dec.T.astype(BF16)
        sub = lax.broadcasted_iota(jnp.int32, (8, R), 0)
        gl_ref[:, cs:cs + HEAD_DIM] = jnp.where(
            sub == 0, jnp.exp(g_last0), jnp.where(sub == 1, jnp.exp(g_last1), 0.0))


def _gdn_prep(proj, conv_w, gates, gates_t, seq_len):
    T = proj.shape[0]
    R = GDN_TILE
    n_tiles = T // R
    q_blk, k_blk, v_blk = BLK_GQ, BLK_GK, BLK_GV
    per_tile = R // HALO

    def main_spec(cb):
        return pl.BlockSpec((R, HEADS_W), lambda i: (i, cb))

    def halo_spec(cb):
        return pl.BlockSpec((HALO, HEADS_W), lambda i: (jnp.maximum(i * per_tile - 1, 0), cb))

    wide = pl.BlockSpec((R, HEADS_W), lambda i: (i, 0))
    return pl.pallas_call(
        functools.partial(_gdn_prep_kernel, tiles_per_seq=seq_len // R),
        grid=(n_tiles,),
        in_specs=[
            main_spec(q_blk), main_spec(k_blk), main_spec(v_blk),
            halo_spec(q_blk), halo_spec(k_blk), halo_spec(v_blk),
            pl.BlockSpec((CONV_WIDTH, 3 * HEADS_W), lambda i: (0, 0)),
            pl.BlockSpec((R, LANES), lambda i: (i, 0)),
            pl.BlockSpec((32, R), lambda i: (0, i)),
        ],
        out_specs=[
            wide, wide, wide, wide,
            pl.BlockSpec((HEADS_W, R), lambda i: (0, i)),
            pl.BlockSpec((8, HEADS_W), lambda i: (i, 0)),
        ],
        out_shape=[
            jax.ShapeDtypeStruct((T, HEADS_W), BF16),
            jax.ShapeDtypeStruct((T, HEADS_W), BF16),
            jax.ShapeDtypeStruct((T, HEADS_W), BF16),
            jax.ShapeDtypeStruct((T, HEADS_W), BF16),
            jax.ShapeDtypeStruct((HEADS_W, T), BF16),
            jax.ShapeDtypeStruct((n_tiles * 8, HEADS_W), F32),
        ],
        scratch_shapes=[pltpu.VMEM((3, HALO + R, HEADS_W), F32)],
        compiler_params=_cparams(("arbitrary",)),
        name="gdn_prep",
    )(proj, proj, proj, proj, proj, proj, conv_w, gates, gates_t)


def _gdn_scan_kernel(u_ref, w_ref, qd_ref, at_ref, kdt_ref, gl_ref, z_ref, nw_ref, y_ref, state):
    s = pl.program_id(1)
    rows = u_ref.shape[0]

    @pl.when(s == 0)
    def _():
        state[...] = jnp.zeros_like(state)

    zeros_half = jnp.zeros((GDN_CHUNK, HEAD_DIM), BF16)
    for c in range(rows // GDN_CHUNK):
        r0 = c * GDN_CHUNK
        tile, half = divmod(c, 2)
        t0 = tile * GDN_TILE
        for hh in range(N_HEADS):
            cs = hh * HEAD_DIM
            st = state[hh]
            sb = st.astype(BF16)
            ws = _dot(w_ref[r0:r0 + GDN_CHUNK, cs:cs + HEAD_DIM], sb)
            qs = _dot(qd_ref[r0:r0 + GDN_CHUNK, cs:cs + HEAD_DIM], sb)
            v_new = (u_ref[r0:r0 + GDN_CHUNK, cs:cs + HEAD_DIM].astype(F32) - ws).astype(BF16)
            v_lo = jnp.concatenate([v_new, zeros_half], axis=0)
            v_own = v_lo if half == 0 else jnp.concatenate([zeros_half, v_new], axis=0)
            o = qs + _dot(at_ref[r0:r0 + GDN_CHUNK, cs:cs + HEAD_DIM], v_lo)
            gl = gl_ref[tile * 8 + half:tile * 8 + half + 1, cs:cs + HEAD_DIM]
            state[hh] = st * gl + _dot(kdt_ref[cs:cs + HEAD_DIM, t0:t0 + GDN_TILE], v_own)

            ms = jnp.mean(o * o, axis=-1, keepdims=True)
            z = z_ref[r0:r0 + GDN_CHUNK, cs:cs + HEAD_DIM].astype(F32)
            y = o * lax.rsqrt(ms + RMS_EPS) * nw_ref[...] * _silu(z)
            y_ref[r0:r0 + GDN_CHUNK, cs:cs + HEAD_DIM] = y.astype(y_ref.dtype)


def _gdn_scan(u, w, qd, at, kdt, gl, proj, out_norm_w, batch, seq_len):
    T = u.shape[0]
    rows = min(256, seq_len)
    ns = seq_len // rows
    z_blk = BLK_GZ
    wide = pl.BlockSpec((rows, HEADS_W), lambda b, s: (b * ns + s, 0))
    gl_rows = rows // GDN_TILE * 8
    return pl.pallas_call(
        _gdn_scan_kernel,
        grid=(batch, ns),
        in_specs=[
            wide, wide, wide, wide,
            pl.BlockSpec((HEADS_W, rows), lambda b, s: (0, b * ns + s)),
            pl.BlockSpec((gl_rows, HEADS_W), lambda b, s: (b * ns + s, 0)),
            pl.BlockSpec((rows, HEADS_W), lambda b, s: (b * ns + s, z_blk)),
            pl.BlockSpec((1, HEAD_DIM), lambda b, s: (0, 0)),
        ],
        out_specs=wide,
        out_shape=jax.ShapeDtypeStruct((T, HEADS_W), BF16),
        scratch_shapes=[pltpu.VMEM((N_HEADS, HEAD_DIM, HEAD_DIM), F32)],
        compiler_params=_cparams(("arbitrary", "arbitrary")),
        name="gdn_scan",
    )(u, w, qd, at, kdt, gl, proj, out_norm_w)


def _merge_kernel(yf_ref, yg_ref, ga_ref, gb_ref, x_ref, wf_ref, wg_ref, wo_ref, n2_ref,
                  wr_ref, br_ref, x2_ref, h2_ref, lg_ref):
    a = _dot(yf_ref[...], wf_ref[...])
    b = _dot(yg_ref[...], wg_ref[...])
    merged = (jax.nn.sigmoid(ga_ref[...].astype(F32)) * a
              + jax.nn.sigmoid(gb_ref[...].astype(F32)) * b)
    x2 = x_ref[...] + _dot(merged.astype(BF16), wo_ref[...])
    x2_ref[...] = x2
    ms = jnp.mean(x2 * x2, axis=-1, keepdims=True)
    h2 = x2 * lax.rsqrt(ms + RMS_EPS) * n2_ref[...]
    h2_ref[...] = h2
    lg_ref[...] = _dot_exact(h2, wr_ref[...]) + br_ref[...]


def _merge(y_fox, y_gdn, proj, x2d, wf, wg, wo, n2, wr, br):
    T, D = x2d.shape
    tm = min(256, T)
    ga_blk, gb_blk = BLK_GATE_A * HEADS_W // D, BLK_GATE_B * HEADS_W // D
    const = lambda shape: pl.BlockSpec(shape, lambda i: (0, 0), pipeline_mode=pl.Buffered(1))
    return pl.pallas_call(
        _merge_kernel,
        grid=(T // tm,),
        in_specs=[
            pl.BlockSpec((tm, HEADS_W), lambda i: (i, 0)),
            pl.BlockSpec((tm, HEADS_W), lambda i: (i, 0)),
            pl.BlockSpec((tm, D), lambda i: (i, ga_blk)),
            pl.BlockSpec((tm, D), lambda i: (i, gb_blk)),
            pl.BlockSpec((tm, D), lambda i: (i, 0)),
            const((HEADS_W, D)), const((HEADS_W, D)), const((D, D)), const((1, D)),
            const((D, LANES)), const((1, LANES)),
        ],
        out_specs=[
            pl.BlockSpec((tm, D), lambda i: (i, 0)),
            pl.BlockSpec((tm, D), lambda i: (i, 0)),
            pl.BlockSpec((tm, LANES), lambda i: (i, 0)),
        ],
        out_shape=[
            jax.ShapeDtypeStruct((T, D), F32),
            jax.ShapeDtypeStruct((T, D), F32),
            jax.ShapeDtypeStruct((T, LANES), F32),
        ],
        compiler_params=_cparams(("arbitrary",)),
        name="merge",
    )(y_fox, y_gdn, proj, proj, x2d, wf, wg, wo, n2, wr, br)


LANE_RANK = TOP_K


def _route_kernel(lg_ref, ri_ref, rw_ref, cnt_ref, carry):
    i = pl.program_id(0)
    tm = lg_ref.shape[0]

    @pl.when(i == 0)
    def _():
        carry[...] = jnp.zeros_like(carry)

    lane = lax.broadcasted_iota(jnp.int32, (tm, LANES), 1)
    lane_f = lane.astype(F32)
    v = jnp.where(lane < N_EXPERTS, lg_ref[...], -jnp.inf)
    vals, idxs, hots = [], [], []
    for _ in range(TOP_K):
        m = jnp.max(v, axis=1, keepdims=True)
        idx = jnp.min(jnp.where(v == m, lane_f, float(LANES)), axis=1, keepdims=True)
        hot = lane_f == idx
        vals.append(m)
        idxs.append(idx)
        hots.append(hot)
        v = jnp.where(hot, -jnp.inf, v)

    exps = [jnp.exp(val - vals[0]) for val in vals]
    den = exps[0] + exps[1] + exps[2] + exps[3]

    multi_hot = jnp.zeros((tm, LANES), F32)
    for hot in hots:
        multi_hot = multi_hot + hot.astype(F32)
    row = lax.broadcasted_iota(jnp.int32, (tm, tm), 0)
    col = lax.broadcasted_iota(jnp.int32, (tm, tm), 1)
    before = _dot((row > col).astype(BF16), multi_hot.astype(BF16)) + carry[0:1, :]
    total = before[tm - 1:tm, :] + multi_hot[tm - 1:tm, :]
    carry[...] = jnp.broadcast_to(total, carry.shape)
    cnt_ref[...] = jnp.broadcast_to(total, cnt_ref.shape)

    out_i = jnp.zeros((tm, LANES), jnp.int32)
    out_w = jnp.zeros((tm, LANES), F32)
    for k in range(TOP_K):
        rank = jnp.sum(jnp.where(hots[k], before, 0.0), axis=1, keepdims=True).astype(jnp.int32)
        out_i = jnp.where(lane == k, idxs[k].astype(jnp.int32), out_i)
        out_i = jnp.where(lane == LANE_RANK + k, rank, out_i)
        out_w = jnp.where(lane == k, exps[k] / den, out_w)
    ri_ref[...] = out_i
    rw_ref[...] = out_w


def _route(logits):
    T = logits.shape[0]
    tm = min(256, T)
    return pl.pallas_call(
        _route_kernel,
        grid=(T // tm,),
        in_specs=[pl.BlockSpec((tm, LANES), lambda i: (i, 0))],
        out_specs=[
            pl.BlockSpec((tm, LANES), lambda i: (i, 0)),
            pl.BlockSpec((tm, LANES), lambda i: (i, 0)),
            pl.BlockSpec((8, LANES), lambda i: (0, 0)),
        ],
        out_shape=[
            jax.ShapeDtypeStruct((T, LANES), jnp.int32),
            jax.ShapeDtypeStruct((T, LANES), F32),
            jax.ShapeDtypeStruct((8, LANES), F32),
        ],
        scratch_shapes=[pltpu.VMEM((8, LANES), F32)],
        compiler_params=_cparams(("arbitrary",)),
        name="route",
    )(logits)


DISPATCH_TOKENS = 256
ROW_DMA_UNROLL = 4


def _dispatch_kernel(dest_ref, h_ref, xs_in_ref, xs_ref, sem):
    del xs_in_ref

    def row_copy(r, k):
        return pltpu.make_async_copy(
            h_ref.at[pl.ds(r, 1)], xs_ref.at[pl.ds(dest_ref[r * TOP_K + k], 1)], sem)

    def start(r, carry):
        for k in range(TOP_K):
            row_copy(r, k).start()
        return carry

    def wait(r, carry):
        for k in range(TOP_K):
            row_copy(r, k).wait()
        return carry

    lax.fori_loop(0, DISPATCH_TOKENS, start, 0, unroll=ROW_DMA_UNROLL)
    lax.fori_loop(0, DISPATCH_TOKENS, wait, 0, unroll=ROW_DMA_UNROLL)


def _dispatch(dest_flat, h2, xs_zero):
    T, D = h2.shape
    n = DISPATCH_TOKENS * TOP_K
    return pl.pallas_call(
        _dispatch_kernel,
        grid=(T // DISPATCH_TOKENS,),
        in_specs=[
            pl.BlockSpec((n,), lambda i: (i,), memory_space=pltpu.SMEM),
            pl.BlockSpec((DISPATCH_TOKENS, D), lambda i: (i, 0)),
            pl.BlockSpec(memory_space=pl.ANY),
        ],
        out_specs=pl.BlockSpec(memory_space=pl.ANY),
        out_shape=jax.ShapeDtypeStruct(xs_zero.shape, xs_zero.dtype),
        scratch_shapes=[pltpu.SemaphoreType.DMA(())],
        input_output_aliases={2: 0},
        compiler_params=pltpu.CompilerParams(
            dimension_semantics=("arbitrary",), has_side_effects=True),
        name="dispatch",
    )(dest_flat, h2, xs_zero)


def _moe_up_kernel(be_ref, nu_ref, x_ref, wg_ref, wl_ref, bg_ref, bl_ref, act_ref):
    b = pl.program_id(1)

    @pl.when(b < nu_ref[0])
    def _():
        x = x_ref[...].astype(BF16)
        gate = jnp.minimum(_dot(x, wg_ref[...]) + bg_ref[...], SWIGLU_LIMIT)
        lin = jnp.clip(_dot(x, wl_ref[...]) + bl_ref[...], -SWIGLU_LIMIT, SWIGLU_LIMIT)
        act = gate * jax.nn.sigmoid(SWIGLU_ALPHA * gate) * (lin + 1.0)
        act_ref[...] = act.astype(act_ref.dtype)

    @pl.when(b >= nu_ref[0])
    def _():
        act_ref[...] = jnp.zeros_like(act_ref)


def _moe_up(block_expert, n_used, xs, w_gate, w_lin, b_gate, b_lin):
    P, D = xs.shape
    dff = w_gate.shape[2]
    tn = 512
    nb = P // MOE_ROWS

    def blk(b, nu):
        return jnp.minimum(b, nu[0] - 1)

    grid_spec = pltpu.PrefetchScalarGridSpec(
        num_scalar_prefetch=2,
        grid=(dff // tn, nb),
        in_specs=[
            pl.BlockSpec((MOE_ROWS, D), lambda j, b, be, nu: (blk(b, nu), 0)),
            pl.BlockSpec((None, D, tn), lambda j, b, be, nu: (be[blk(b, nu)], 0, j)),
            pl.BlockSpec((None, D, tn), lambda j, b, be, nu: (be[blk(b, nu)], 0, j)),
            pl.BlockSpec((None, 1, tn), lambda j, b, be, nu: (be[blk(b, nu)], 0, j)),
            pl.BlockSpec((None, 1, tn), lambda j, b, be, nu: (be[blk(b, nu)], 0, j)),
        ],
        out_specs=pl.BlockSpec((MOE_ROWS, tn), lambda j, b, be, nu: (b, j)),
    )
    return pl.pallas_call(
        _moe_up_kernel,
        grid_spec=grid_spec,
        out_shape=jax.ShapeDtypeStruct((P, dff), BF16),
        compiler_params=_cparams(("arbitrary", "arbitrary")),
        name="moe_up",
    )(block_expert, n_used, xs, w_gate, w_lin, b_gate, b_lin)


def _moe_down_kernel(be_ref, nu_ref, a_ref, wd_ref, bd_ref, y_ref):
    b = pl.program_id(1)

    @pl.when(b < nu_ref[0])
    def _():
        y_ref[...] = _dot(a_ref[...], wd_ref[...]) + bd_ref[...]

    @pl.when(b >= nu_ref[0])
    def _():
        y_ref[...] = jnp.zeros_like(y_ref)


def _moe_down(block_expert, n_used, act, w_down, b_down):
    P, dff = act.shape
    D = w_down.shape[2]
    tn = 1024
    nb = P // MOE_ROWS

    def blk(b, nu):
        return jnp.minimum(b, nu[0] - 1)

    grid_spec = pltpu.PrefetchScalarGridSpec(
        num_scalar_prefetch=2,
        grid=(D // tn, nb),
        in_specs=[
            pl.BlockSpec((MOE_ROWS, dff), lambda j, b, be, nu: (blk(b, nu), 0)),
            pl.BlockSpec((None, dff, tn), lambda j, b, be, nu: (be[blk(b, nu)], 0, j)),
            pl.BlockSpec((None, 1, tn), lambda j, b, be, nu: (be[blk(b, nu)], 0, j)),
        ],
        out_specs=pl.BlockSpec((MOE_ROWS, tn), lambda j, b, be, nu: (b, j)),
    )
    return pl.pallas_call(
        _moe_down_kernel,
        grid_spec=grid_spec,
        out_shape=jax.ShapeDtypeStruct((P, D), F32),
        compiler_params=_cparams(("arbitrary", "arbitrary")),
        name="moe_down",
    )(block_expert, n_used, act, w_down, b_down)


COMBINE_TOKENS = 256


def _combine_kernel(dest_ref, y_ref, x2_ref, rw_ref, o_ref, buf, sem):
    def row_copy(r, k):
        return pltpu.make_async_copy(
            y_ref.at[pl.ds(dest_ref[r * TOP_K + k], 1)], buf.at[k, pl.ds(r, 1)], sem)

    def start(r, carry):
        for k in range(TOP_K):
            row_copy(r, k).start()
        return carry

    def wait(r, carry):
        for k in range(TOP_K):
            row_copy(r, k).wait()
        return carry

    lax.fori_loop(0, COMBINE_TOKENS, start, 0, unroll=ROW_DMA_UNROLL)
    lax.fori_loop(0, COMBINE_TOKENS, wait, 0, unroll=ROW_DMA_UNROLL)
    acc = x2_ref[...]
    for k in range(TOP_K):
        acc = acc + rw_ref[:, k:k + 1] * buf[k]
    o_ref[...] = acc


def _combine(dest_flat, y, x2, route_w):
    T, D = x2.shape
    tc = COMBINE_TOKENS
    return pl.pallas_call(
        _combine_kernel,
        grid=(T // tc,),
        in_specs=[
            pl.BlockSpec((tc * TOP_K,), lambda i: (i,), memory_space=pltpu.SMEM),
            pl.BlockSpec(memory_space=pl.ANY),
            pl.BlockSpec((tc, D), lambda i: (i, 0)),
            pl.BlockSpec((tc, LANES), lambda i: (i, 0)),
        ],
        out_specs=pl.BlockSpec((tc, D), lambda i: (i, 0)),
        out_shape=jax.ShapeDtypeStruct((T, D), F32),
        scratch_shapes=[pltpu.VMEM((TOP_K, tc, D), F32), pltpu.SemaphoreType.DMA(())],
        compiler_params=_cparams(("arbitrary",)),
        name="combine",
    )(dest_flat, y, x2, route_w)


def _pad_lanes(v, offset):
    return jnp.zeros((LANES,), F32).at[offset:offset + v.shape[0]].set(v.astype(F32))


def _layer(x, norm1_w, w_in, b_fox_f, fox_q_norm_w, fox_k_norm_w, gdn_conv_w, gdn_A_log,
           gdn_dt_bias, gdn_out_norm_w, w_up_fox, w_up_gdn, w_o, norm2_w, w_router, b_router,
           w_gate_up, b_gate_up, w_down, b_down):
    B, S, D = x.shape
    T = B * S
    x2d = x.reshape(T, D)

    o_f = 3 * HEADS_W
    o_gqkv = o_f + N_HEADS
    o_gb = o_gqkv + 4 * HEADS_W
    o_gate = o_gb + 2 * N_HEADS
    w_big = jnp.concatenate(
        [w_in[:, o_gate:], w_in[:, :o_f], w_in[:, o_gqkv:o_gb]], axis=1).astype(BF16)
    w_small = jnp.concatenate(
        [w_in[:, o_f:o_gqkv], w_in[:, o_gb:o_gate],
         jnp.zeros((D, LANES - 3 * N_HEADS), w_in.dtype)], axis=1).astype(BF16)

    proj, small, v_t = _in_proj(x2d, norm1_w.reshape(1, D), w_big, w_small,
                                fox_q_norm_w.reshape(1, HEAD_DIM), fox_k_norm_w.reshape(1, HEAD_DIM))

    gate_params = jnp.zeros((8, LANES), F32)
    gate_params = gate_params.at[0].set(_pad_lanes(b_fox_f, LANE_C) + _pad_lanes(gdn_dt_bias, LANE_GC))
    gate_params = gate_params.at[1].set(_pad_lanes(gdn_A_log, LANE_GC))
    gates = _gates(small, gate_params, S)
    gates_t = gates[:, :32].T
    c_bcast = jnp.broadcast_to(
        (gates_t[LANE_C:LANE_C + N_HEADS] * LOG2E)[:, :, None], (N_HEADS, T, LANES))

    y_fox = _fox(proj, v_t, c_bcast, B, S)

    u, w, qd, at, kdt, gl = _gdn_prep(proj, gdn_conv_w.astype(F32), gates, gates_t, S)
    y_gdn = _gdn_scan(u, w, qd, at, kdt, gl, proj, gdn_out_norm_w.reshape(1, HEAD_DIM), B, S)

    w_router_p = jnp.zeros((D, LANES), F32).at[:, :N_EXPERTS].set(w_router.astype(F32))
    b_router_p = _pad_lanes(b_router, 0).reshape(1, LANES)
    x2, h2, logits = _merge(y_fox, y_gdn, proj, x2d, w_up_fox.astype(BF16), w_up_gdn.astype(BF16),
                            w_o.astype(BF16), norm2_w.reshape(1, D), w_router_p, b_router_p)

    route_i, route_w, counts = _route(logits)
    expert = route_i[:, :TOP_K]
    rank = route_i[:, LANE_RANK:LANE_RANK + TOP_K]
    counts = counts[0, :N_EXPERTS].astype(jnp.int32)
    padded = (counts + MOE_ROWS - 1) // MOE_ROWS * MOE_ROWS
    pad_end = jnp.cumsum(padded)
    pad_start = pad_end - padded
    dest_flat = (pad_start[expert] + rank).reshape(T * TOP_K).astype(jnp.int32)
    n_blocks = -(-T * TOP_K // MOE_ROWS) + N_EXPERTS
    n_used = (pad_end[-1:] // MOE_ROWS).astype(jnp.int32)
    block_expert = jnp.minimum(
        jnp.searchsorted(pad_end, jnp.arange(n_blocks, dtype=jnp.int32) * MOE_ROWS, side="right"),
        N_EXPERTS - 1).astype(jnp.int32)

    xs = _dispatch(dest_flat, h2, jnp.zeros((n_blocks * MOE_ROWS, D), F32))

    dff = w_down.shape[1]
    w_gu = w_gate_up.reshape(N_EXPERTS, D, dff, 2)
    b_gu = b_gate_up.reshape(N_EXPERTS, 1, dff, 2)
    act = _moe_up(block_expert, n_used, xs, w_gu[..., 0].astype(BF16), w_gu[..., 1].astype(BF16),
                  b_gu[..., 0].astype(F32), b_gu[..., 1].astype(F32))
    y = _moe_down(block_expert, n_used, act, w_down.astype(BF16),
                  b_down.reshape(N_EXPERTS, 1, D).astype(F32))

    out = _combine(dest_flat, y, x2, route_w)
    return out.reshape(B, S, D)


def kernel(x, norm1_w, w_in, b_fox_f, fox_q_norm_w, fox_k_norm_w, gdn_conv_w, gdn_A_log, gdn_dt_bias, gdn_out_norm_w, w_up_fox, w_up_gdn, w_o, norm2_w, w_router, b_router, w_gate_up, b_gate_up, w_down, b_down):
    depth = norm1_w.shape[0]
    for l in range(depth):
        x = _layer(x, norm1_w[l], w_in[l], b_fox_f[l], fox_q_norm_w[l], fox_k_norm_w[l],
                   gdn_conv_w[l], gdn_A_log[l], gdn_dt_bias[l], gdn_out_norm_w[l], w_up_fox[l],
                   w_up_gdn[l], w_o[l], norm2_w[l], w_router[l], b_router[l], w_gate_up[l],
                   b_gate_up[l], w_down[l], b_down[l])
    return x
```

```python
import functools

import jax
import jax.numpy as jnp
from jax import lax
from jax.experimental import pallas as pl
from jax.experimental.pallas import tpu as pltpu

F32 = jnp.float32
BF16 = jnp.bfloat16

N_HEADS = 8
HEAD_DIM = 128
HEADS_W = N_HEADS * HEAD_DIM
GDN_CHUNK = 64
CONV_WIDTH = 4
N_EXPERTS = 32
TOP_K = 4
SWIGLU_LIMIT = 7.0
SWIGLU_ALPHA = 1.702
RMS_EPS = 1e-6
LANES = 128
NEG_BIG = -1e30
LOG2E = 1.4426950408889634

VMEM_LIMIT = 56 * 1024 * 1024

MOE_ROWS = 512


def _cparams(sem):
    return pltpu.CompilerParams(dimension_semantics=sem, vmem_limit_bytes=VMEM_LIMIT)


def _nt_dot(a, b):
    return lax.dot_general(a, b, (((1,), (1,)), ((), ())), preferred_element_type=F32)


def _dot(a, b):
    return jnp.dot(a, b, preferred_element_type=F32)


def _dot_exact(a, b):
    return jnp.dot(a, b, preferred_element_type=F32, precision=lax.Precision.HIGHEST)


def _silu(x):
    return x * jax.nn.sigmoid(x)


BLK_GATE_A, BLK_GATE_B = 0, 2
BLK_FQ, BLK_FK, BLK_FV = 4, 5, 6
BLK_GQ, BLK_GK, BLK_GV, BLK_GZ = 7, 8, 9, 10
CHUNK_SHIFT = GDN_CHUNK.bit_length() - 1


def _in_proj_kernel(x_ref, n1_ref, w_ref, ws_ref, qn_ref, kn_ref, proj_ref, small_ref, vt_ref,
                    h_scr):
    j = pl.program_id(1)

    @pl.when(j == 0)
    def _():
        x = x_ref[...]
        ms = jnp.mean(x * x, axis=-1, keepdims=True)
        h = (x * lax.rsqrt(ms + RMS_EPS) * n1_ref[...]).astype(BF16)
        h_scr[...] = h
        small_ref[...] = _dot(h, ws_ref[...])

    acc = _dot(h_scr[...], w_ref[...])

    def head_norm(nw_ref, scale):
        for hh in range(N_HEADS):
            a = acc[:, hh * HEAD_DIM:(hh + 1) * HEAD_DIM]
            ms = jnp.mean(a * a, axis=-1, keepdims=True)
            y = a * lax.rsqrt(ms + RMS_EPS) * nw_ref[...] * scale
            proj_ref[:, hh * HEAD_DIM:(hh + 1) * HEAD_DIM] = y.astype(BF16)

    @pl.when(j == BLK_FQ)
    def _():
        head_norm(qn_ref, HEAD_DIM ** -0.5 * LOG2E)

    @pl.when(j == BLK_FK)
    def _():
        head_norm(kn_ref, 1.0)

    @pl.when(j == BLK_FV)
    def _():
        vt_ref[...] = acc.T.astype(BF16)

    @pl.when(jnp.logical_and(j != BLK_FQ, j != BLK_FK))
    def _():
        proj_ref[...] = acc.astype(BF16)


def _in_proj(x2d, n1, w_big, w_small, qn, kn):
    T, D = x2d.shape
    n_cols = w_big.shape[1]
    tm = min(512, T)
    tn = HEADS_W
    return pl.pallas_call(
        _in_proj_kernel,
        grid=(T // tm, n_cols // tn),
        in_specs=[
            pl.BlockSpec((tm, D), lambda i, j: (i, 0)),
            pl.BlockSpec((1, D), lambda i, j: (0, 0)),
            pl.BlockSpec((D, tn), lambda i, j: (0, j)),
            pl.BlockSpec((D, LANES), lambda i, j: (0, 0)),
            pl.BlockSpec((1, HEAD_DIM), lambda i, j: (0, 0)),
            pl.BlockSpec((1, HEAD_DIM), lambda i, j: (0, 0)),
        ],
        out_specs=[
            pl.BlockSpec((tm, tn), lambda i, j: (i, j)),
            pl.BlockSpec((tm, LANES), lambda i, j: (i, 0)),
            pl.BlockSpec((HEADS_W, tm), lambda i, j: (0, i)),
        ],
        out_shape=[
            jax.ShapeDtypeStruct((T, n_cols), BF16),
            jax.ShapeDtypeStruct((T, LANES), F32),
            jax.ShapeDtypeStruct((HEADS_W, T), BF16),
        ],
        scratch_shapes=[pltpu.VMEM((tm, D), BF16)],
        compiler_params=_cparams(("arbitrary", "arbitrary")),
        name="in_proj",
    )(x2d, n1, w_big, w_small, qn, kn)


LANE_C, LANE_BETA, LANE_GC = 0, N_HEADS, 2 * N_HEADS


def _gates_kernel(s_ref, p_ref, g_ref, carry, *, tiles_per_seq):
    i = pl.program_id(0)
    tg = s_ref.shape[0]

    @pl.when(i % tiles_per_seq == 0)
    def _():
        carry[...] = jnp.zeros_like(carry)

    z = s_ref[...] + p_ref[0:1, :]
    soft = jnp.log1p(jnp.exp(-jnp.abs(z)))
    log_sig = jnp.minimum(z, 0.0) - soft
    softplus = jnp.maximum(z, 0.0) + soft
    beta = jax.nn.sigmoid(z)
    g = -jnp.exp(p_ref[1:2, :]) * softplus

    row = lax.broadcasted_iota(jnp.int32, (tg, tg), 0)
    col = lax.broadcasted_iota(jnp.int32, (tg, tg), 1)
    tri = row >= col
    tri_chunk = jnp.logical_and(tri, (row >> CHUNK_SHIFT) == (col >> CHUNK_SHIFT))
    c = _dot_exact(tri.astype(F32), log_sig) + carry[...]
    gc = _dot_exact(tri_chunk.astype(F32), g)
    carry[...] = c[tg - 1:tg, :]

    lane = lax.broadcasted_iota(jnp.int32, (tg, LANES), 1)
    g_ref[...] = jnp.where(lane < LANE_BETA, c,
                           jnp.where(lane < LANE_GC, beta,
                                     jnp.where(lane < LANE_GC + N_HEADS, gc, 0.0)))


def _gates(small, params, seq_len):
    T = small.shape[0]
    tg = min(256, seq_len)
    return pl.pallas_call(
        functools.partial(_gates_kernel, tiles_per_seq=seq_len // tg),
        grid=(T // tg,),
        in_specs=[
            pl.BlockSpec((tg, LANES), lambda i: (i, 0)),
            pl.BlockSpec((8, LANES), lambda i: (0, 0)),
        ],
        out_specs=pl.BlockSpec((tg, LANES), lambda i: (i, 0)),
        out_shape=jax.ShapeDtypeStruct((T, LANES), F32),
        scratch_shapes=[pltpu.VMEM((1, LANES), F32)],
        compiler_params=_cparams(("arbitrary",)),
        name="gates",
    )(small, params)


FOX_TILE = 1024


def _fox_kernel(qi_ref, kj_ref, q_ref, k_ref, vt_ref, cb_ref, o_ref, m_scr, l_scr, acc_scr):
    p = pl.program_id(2)
    i = qi_ref[p]
    j = kj_ref[p]
    tq, tk = q_ref.shape[0], k_ref.shape[0]

    @pl.when(j == 0)
    def _():
        m_scr[...] = jnp.full_like(m_scr, NEG_BIG)
        l_scr[...] = jnp.zeros_like(l_scr)
        acc_scr[...] = jnp.zeros_like(acc_scr)

    def step(masked):
        st = _nt_dot(k_ref[...], q_ref[...])
        st = st - jnp.concatenate([cb_ref[...]] * (tq // LANES), axis=1)
        if masked:
            key = lax.broadcasted_iota(jnp.int32, (tk, tq), 0)
            qry = lax.broadcasted_iota(jnp.int32, (tk, tq), 1)
            st = jnp.where(qry >= key, st, -jnp.inf)
        m_old = m_scr[...]
        m_new = jnp.maximum(m_old, jnp.max(st, axis=0, keepdims=True))
        alpha = jnp.exp2(m_old - m_new)
        pt = jnp.exp2(st - m_new)
        l_scr[...] = alpha * l_scr[...] + jnp.sum(pt, axis=0, keepdims=True)
        acc_scr[...] = alpha * acc_scr[...] + _dot(vt_ref[...], pt.astype(BF16))
        m_scr[...] = m_new

    @pl.when(j < i)
    def _():
        step(False)

    @pl.when(j == i)
    def _():
        step(True)
        o_ref[...] = (acc_scr[...] / l_scr[...]).T.astype(o_ref.dtype)


def _fox(proj, v_t, c_bcast, batch, seq_len):
    T = proj.shape[0]
    tq = min(FOX_TILE, seq_len)
    nq = seq_len // tq
    pairs = [(i, j) for i in range(nq) for j in range(i + 1)]
    qi = jnp.asarray([p[0] for p in pairs], jnp.int32)
    kj = jnp.asarray([p[1] for p in pairs], jnp.int32)
    q_off, k_off = BLK_FQ * N_HEADS, BLK_FK * N_HEADS
    grid_spec = pltpu.PrefetchScalarGridSpec(
        num_scalar_prefetch=2,
        grid=(batch, N_HEADS, len(pairs)),
        in_specs=[
            pl.BlockSpec((tq, HEAD_DIM), lambda b, h, p, qi, kj: (b * nq + qi[p], q_off + h)),
            pl.BlockSpec((tq, HEAD_DIM), lambda b, h, p, qi, kj: (b * nq + kj[p], k_off + h)),
            pl.BlockSpec((HEAD_DIM, tq), lambda b, h, p, qi, kj: (h, b * nq + kj[p])),
            pl.BlockSpec((None, tq, LANES), lambda b, h, p, qi, kj: (h, b * nq + kj[p], 0)),
        ],
        out_specs=pl.BlockSpec((tq, HEAD_DIM), lambda b, h, p, qi, kj: (b * nq + qi[p], h)),
        scratch_shapes=[
            pltpu.VMEM((1, tq), F32),
            pltpu.VMEM((1, tq), F32),
            pltpu.VMEM((HEAD_DIM, tq), F32),
        ],
    )
    return pl.pallas_call(
        _fox_kernel,
        grid_spec=grid_spec,
        out_shape=jax.ShapeDtypeStruct((T, HEADS_W), BF16),
        compiler_params=_cparams(("arbitrary", "arbitrary", "arbitrary")),
        name="fox",
    )(qi, kj, proj, proj, v_t, c_bcast)


GDN_TILE = 2 * GDN_CHUNK
HALO = 16


def _gdn_prep_kernel(q_ref, k_ref, v_ref, hq_ref, hk_ref, hv_ref, cw_ref, g_ref, gt_ref,
                     u_ref, w_ref, qd_ref, at_ref, kdt_ref, gl_ref, *, tiles_per_seq):
    i = pl.program_id(0)
    R = GDN_TILE
    first = (i % tiles_per_seq) == 0
    heads = range(N_HEADS)

    def hslice(x, hh):
        return x[:, hh * HEAD_DIM:(hh + 1) * HEAD_DIM]

    dst_t = lax.broadcasted_iota(jnp.int32, (R, 2 * R), 0)
    src_t = lax.broadcasted_iota(jnp.int32, (R, 2 * R), 1)
    src_ok = jnp.logical_or(src_t >= HALO, jnp.logical_not(first))
    shifts = [jnp.logical_and(src_t == dst_t + HALO - back, src_ok).astype(BF16)
              for back in range(CONV_WIDTH - 1, 0, -1)]
    zero_rows = jnp.zeros((2 * R - HALO - R, HEADS_W), BF16)

    def conv_silu(grp, m_ref, h_ref):
        main = m_ref[...]
        stacked = jnp.concatenate([h_ref[...], main, zero_rows], axis=0)
        cw = cw_ref[:, grp * HEADS_W:(grp + 1) * HEADS_W]
        acc = main.astype(F32) * cw[CONV_WIDTH - 1:CONV_WIDTH, :]
        for tap in range(CONV_WIDTH - 1):
            acc = acc + _dot(shifts[tap], stacked) * cw[tap:tap + 1, :]
        return _silu(acc)

    q_all = conv_silu(0, q_ref, hq_ref)
    k_all = conv_silu(1, k_ref, hk_ref)
    v_all = conv_silu(2, v_ref, hv_ref)

    def l2n(x):
        return x * lax.rsqrt(jnp.sum(x * x, axis=-1, keepdims=True) + RMS_EPS)

    qn = [l2n(hslice(q_all, hh)) * (HEAD_DIM ** -0.5) for hh in heads]
    kn = [l2n(hslice(k_all, hh)) for hh in heads]
    qb = [x.astype(BF16) for x in qn]
    kb = [x.astype(BF16) for x in kn]
    vb = [hslice(v_all, hh).astype(BF16) for hh in heads]

    row = lax.broadcasted_iota(jnp.int32, (R, R), 0)
    col = lax.broadcasted_iota(jnp.int32, (R, R), 1)
    same = (row >> CHUNK_SHIFT) == (col >> CHUNK_SHIFT)
    incl = jnp.logical_and(same, row >= col)
    strict = jnp.logical_and(same, row > col)
    eye = (row == col).astype(F32)
    top = row < GDN_CHUNK

    gcol = [jnp.broadcast_to(g_ref[:, LANE_GC + hh:LANE_GC + hh + 1], (R, R)) for hh in heads]
    bcol = [jnp.broadcast_to(g_ref[:, LANE_BETA + hh:LANE_BETA + hh + 1], (R, R)) for hh in heads]
    grow = [gt_ref[LANE_GC + hh:LANE_GC + hh + 1, :] for hh in heads]
    brow = [gt_ref[LANE_BETA + hh:LANE_BETA + hh + 1, :] for hh in heads]

    decay = [jnp.where(incl, jnp.exp(jnp.minimum(gcol[hh] - grow[hh], 0.0)), 0.0) for hh in heads]
    kk = [_nt_dot(kb[hh], kb[hh]) for hh in heads]
    qk = [_nt_dot(qb[hh], kb[hh]) for hh in heads]
    lmat = [jnp.where(strict, kk[hh] * bcol[hh] * decay[hh], 0.0) for hh in heads]

    inv = [eye - lm for lm in lmat]
    power = lmat
    for _ in range(GDN_CHUNK.bit_length() - 2):
        pb = [pw.astype(BF16) for pw in power]
        power = [_dot(x, x) for x in pb]
        inv = [iv + _dot(iv.astype(BF16), pw.astype(BF16)) for iv, pw in zip(inv, power)]

    sub = lax.broadcasted_iota(jnp.int32, (8, R), 0)
    for hh in heads:
        cs = hh * HEAD_DIM
        u = _dot((inv[hh] * brow[hh]).astype(BF16), vb[hh])
        w = _dot((inv[hh] * (brow[hh] * jnp.exp(grow[hh]))).astype(BF16), kb[hh])

        g_last0 = gcol[hh][GDN_CHUNK - 1:GDN_CHUNK, :]
        g_last1 = gcol[hh][R - 1:R, :]
        g_last = jnp.where(top, g_last0, g_last1)
        q_dec = qn[hh] * jnp.exp(gcol[hh])
        k_dec = kn[hh] * jnp.exp(g_last - gcol[hh])

        attn = qk[hh] * decay[hh]
        attn_own = jnp.where(top, attn, pltpu.roll(attn, GDN_CHUNK, 1))
        attn_own = jnp.where(col < GDN_CHUNK, attn_own, 0.0)

        u_ref[:, cs:cs + HEAD_DIM] = u.astype(BF16)
        w_ref[:, cs:cs + HEAD_DIM] = w.astype(BF16)
        qd_ref[:, cs:cs + HEAD_DIM] = q_dec.astype(BF16)
        at_ref[:, cs:cs + HEAD_DIM] = attn_own.astype(BF16)
        kdt_ref[cs:cs + HEAD_DIM, :] = k_dec.T.astype(BF16)
        gl_ref[:, cs:cs + HEAD_DIM] = jnp.where(
            sub == 0, jnp.exp(g_last0), jnp.where(sub == 1, jnp.exp(g_last1), 0.0))


def _gdn_prep(proj, conv_w, gates, gates_t, seq_len):
    T = proj.shape[0]
    R = GDN_TILE
    n_tiles = T // R
    q_blk, k_blk, v_blk = BLK_GQ, BLK_GK, BLK_GV
    per_tile = R // HALO

    def main_spec(cb):
        return pl.BlockSpec((R, HEADS_W), lambda i: (i, cb))

    def halo_spec(cb):
        return pl.BlockSpec((HALO, HEADS_W), lambda i: (jnp.maximum(i * per_tile - 1, 0), cb))

    wide = pl.BlockSpec((R, HEADS_W), lambda i: (i, 0))
    return pl.pallas_call(
        functools.partial(_gdn_prep_kernel, tiles_per_seq=seq_len // R),
        grid=(n_tiles,),
        in_specs=[
            main_spec(q_blk), main_spec(k_blk), main_spec(v_blk),
            halo_spec(q_blk), halo_spec(k_blk), halo_spec(v_blk),
            pl.BlockSpec((CONV_WIDTH, 3 * HEADS_W), lambda i: (0, 0)),
            pl.BlockSpec((R, LANES), lambda i: (i, 0)),
            pl.BlockSpec((32, R), lambda i: (0, i)),
        ],
        out_specs=[
            wide, wide, wide, wide,
            pl.BlockSpec((HEADS_W, R), lambda i: (0, i)),
            pl.BlockSpec((8, HEADS_W), lambda i: (i, 0)),
        ],
        out_shape=[
            jax.ShapeDtypeStruct((T, HEADS_W), BF16),
            jax.ShapeDtypeStruct((T, HEADS_W), BF16),
            jax.ShapeDtypeStruct((T, HEADS_W), BF16),
            jax.ShapeDtypeStruct((T, HEADS_W), BF16),
            jax.ShapeDtypeStruct((HEADS_W, T), BF16),
            jax.ShapeDtypeStruct((n_tiles * 8, HEADS_W), F32),
        ],
        compiler_params=_cparams(("arbitrary",)),
        name="gdn_prep",
    )(proj, proj, proj, proj, proj, proj, conv_w, gates, gates_t)


def _gdn_scan_kernel(u_ref, w_ref, qd_ref, at_ref, kdt_ref, gl_ref, z_ref, nw_ref, y_ref, state):
    s = pl.program_id(1)
    rows = u_ref.shape[0]

    @pl.when(s == 0)
    def _():
        state[...] = jnp.zeros_like(state)

    zeros_half = jnp.zeros((GDN_CHUNK, HEAD_DIM), BF16)
    for c in range(rows // GDN_CHUNK):
        r0 = c * GDN_CHUNK
        tile, half = divmod(c, 2)
        t0 = tile * GDN_TILE
        for hh in range(N_HEADS):
            cs = hh * HEAD_DIM
            st = state[hh]
            sb = st.astype(BF16)
            ws = _dot(w_ref[r0:r0 + GDN_CHUNK, cs:cs + HEAD_DIM], sb)
            qs = _dot(qd_ref[r0:r0 + GDN_CHUNK, cs:cs + HEAD_DIM], sb)
            v_new = (u_ref[r0:r0 + GDN_CHUNK, cs:cs + HEAD_DIM].astype(F32) - ws).astype(BF16)
            v_lo = jnp.concatenate([v_new, zeros_half], axis=0)
            v_own = v_lo if half == 0 else jnp.concatenate([zeros_half, v_new], axis=0)
            o = qs + _dot(at_ref[r0:r0 + GDN_CHUNK, cs:cs + HEAD_DIM], v_lo)
            gl = gl_ref[tile * 8 + half:tile * 8 + half + 1, cs:cs + HEAD_DIM]
            state[hh] = st * gl + _dot(kdt_ref[cs:cs + HEAD_DIM, t0:t0 + GDN_TILE], v_own)

            ms = jnp.mean(o * o, axis=-1, keepdims=True)
            z = z_ref[r0:r0 + GDN_CHUNK, cs:cs + HEAD_DIM].astype(F32)
            y = o * lax.rsqrt(ms + RMS_EPS) * nw_ref[...] * _silu(z)
            y_ref[r0:r0 + GDN_CHUNK, cs:cs + HEAD_DIM] = y.astype(y_ref.dtype)


def _gdn_scan(u, w, qd, at, kdt, gl, proj, out_norm_w, batch, seq_len):
    T = u.shape[0]
    rows = min(256, seq_len)
    ns = seq_len // rows
    z_blk = BLK_GZ
    wide = pl.BlockSpec((rows, HEADS_W), lambda b, s: (b * ns + s, 0))
    gl_rows = rows // GDN_TILE * 8
    return pl.pallas_call(
        _gdn_scan_kernel,
        grid=(batch, ns),
        in_specs=[
            wide, wide, wide, wide,
            pl.BlockSpec((HEADS_W, rows), lambda b, s: (0, b * ns + s)),
            pl.BlockSpec((gl_rows, HEADS_W), lambda b, s: (b * ns + s, 0)),
            pl.BlockSpec((rows, HEADS_W), lambda b, s: (b * ns + s, z_blk)),
            pl.BlockSpec((1, HEAD_DIM), lambda b, s: (0, 0)),
        ],
        out_specs=wide,
        out_shape=jax.ShapeDtypeStruct((T, HEADS_W), BF16),
        scratch_shapes=[pltpu.VMEM((N_HEADS, HEAD_DIM, HEAD_DIM), F32)],
        compiler_params=_cparams(("arbitrary", "arbitrary")),
        name="gdn_scan",
    )(u, w, qd, at, kdt, gl, proj, out_norm_w)


def _merge_kernel(yf_ref, yg_ref, ga_ref, gb_ref, x_ref, wf_ref, wg_ref, wo_ref, n2_ref,
                  wr_ref, br_ref, x2_ref, h2_ref, lg_ref):
    a = _dot(yf_ref[...], wf_ref[...])
    b = _dot(yg_ref[...], wg_ref[...])
    merged = (jax.nn.sigmoid(ga_ref[...].astype(F32)) * a
              + jax.nn.sigmoid(gb_ref[...].astype(F32)) * b)
    x2 = x_ref[...] + _dot(merged.astype(BF16), wo_ref[...])
    x2_ref[...] = x2
    ms = jnp.mean(x2 * x2, axis=-1, keepdims=True)
    h2 = x2 * lax.rsqrt(ms + RMS_EPS) * n2_ref[...]
    h2_ref[...] = h2
    lg_ref[...] = _dot_exact(h2, wr_ref[...]) + br_ref[...]


def _merge(y_fox, y_gdn, proj, x2d, wf, wg, wo, n2, wr, br):
    T, D = x2d.shape
    tm = min(256, T)
    ga_blk, gb_blk = BLK_GATE_A * HEADS_W // D, BLK_GATE_B * HEADS_W // D
    const = lambda shape: pl.BlockSpec(shape, lambda i: (0, 0), pipeline_mode=pl.Buffered(1))
    return pl.pallas_call(
        _merge_kernel,
        grid=(T // tm,),
        in_specs=[
            pl.BlockSpec((tm, HEADS_W), lambda i: (i, 0)),
            pl.BlockSpec((tm, HEADS_W), lambda i: (i, 0)),
            pl.BlockSpec((tm, D), lambda i: (i, ga_blk)),
            pl.BlockSpec((tm, D), lambda i: (i, gb_blk)),
            pl.BlockSpec((tm, D), lambda i: (i, 0)),
            const((HEADS_W, D)), const((HEADS_W, D)), const((D, D)), const((1, D)),
            const((D, LANES)), const((1, LANES)),
        ],
        out_specs=[
            pl.BlockSpec((tm, D), lambda i: (i, 0)),
            pl.BlockSpec((tm, D), lambda i: (i, 0)),
            pl.BlockSpec((tm, LANES), lambda i: (i, 0)),
        ],
        out_shape=[
            jax.ShapeDtypeStruct((T, D), F32),
            jax.ShapeDtypeStruct((T, D), F32),
            jax.ShapeDtypeStruct((T, LANES), F32),
        ],
        compiler_params=_cparams(("arbitrary",)),
        name="merge",
    )(y_fox, y_gdn, proj, proj, x2d, wf, wg, wo, n2, wr, br)


LANE_RANK = TOP_K


def _route_kernel(lg_ref, ri_ref, rw_ref, cnt_ref, carry):
    i = pl.program_id(0)
    tm = lg_ref.shape[0]

    @pl.when(i == 0)
    def _():
        carry[...] = jnp.zeros_like(carry)

    lane = lax.broadcasted_iota(jnp.int32, (tm, LANES), 1)
    lane_f = lane.astype(F32)
    v = jnp.where(lane < N_EXPERTS, lg_ref[...], -jnp.inf)
    vals, idxs, hots = [], [], []
    for _ in range(TOP_K):
        m = jnp.max(v, axis=1, keepdims=True)
        idx = jnp.min(jnp.where(v == m, lane_f, float(LANES)), axis=1, keepdims=True)
        hot = lane_f == idx
        vals.append(m)
        idxs.append(idx)
        hots.append(hot)
        v = jnp.where(hot, -jnp.inf, v)

    exps = [jnp.exp(val - vals[0]) for val in vals]
    den = exps[0] + exps[1] + exps[2] + exps[3]

    multi_hot = jnp.zeros((tm, LANES), F32)
    for hot in hots:
        multi_hot = multi_hot + hot.astype(F32)
    row = lax.broadcasted_iota(jnp.int32, (tm, tm), 0)
    col = lax.broadcasted_iota(jnp.int32, (tm, tm), 1)
    before = _dot((row > col).astype(BF16), multi_hot.astype(BF16)) + carry[0:1, :]
    total = before[tm - 1:tm, :] + multi_hot[tm - 1:tm, :]
    carry[...] = jnp.broadcast_to(total, carry.shape)
    cnt_ref[...] = jnp.broadcast_to(total, cnt_ref.shape)

    out_i = jnp.zeros((tm, LANES), jnp.int32)
    out_w = jnp.zeros((tm, LANES), F32)
    for k in range(TOP_K):
        rank = jnp.sum(jnp.where(hots[k], before, 0.0), axis=1, keepdims=True).astype(jnp.int32)
        out_i = jnp.where(lane == k, idxs[k].astype(jnp.int32), out_i)
        out_i = jnp.where(lane == LANE_RANK + k, rank, out_i)
        out_w = jnp.where(lane == k, exps[k] / den, out_w)
    ri_ref[...] = out_i
    rw_ref[...] = out_w


def _route(logits):
    T = logits.shape[0]
    tm = min(256, T)
    return pl.pallas_call(
        _route_kernel,
        grid=(T // tm,),
        in_specs=[pl.BlockSpec((tm, LANES), lambda i: (i, 0))],
        out_specs=[
            pl.BlockSpec((tm, LANES), lambda i: (i, 0)),
            pl.BlockSpec((tm, LANES), lambda i: (i, 0)),
            pl.BlockSpec((8, LANES), lambda i: (0, 0)),
        ],
        out_shape=[
            jax.ShapeDtypeStruct((T, LANES), jnp.int32),
            jax.ShapeDtypeStruct((T, LANES), F32),
            jax.ShapeDtypeStruct((8, LANES), F32),
        ],
        scratch_shapes=[pltpu.VMEM((8, LANES), F32)],
        compiler_params=_cparams(("arbitrary",)),
        name="route",
    )(logits)


DISPATCH_TOKENS = 256
ROW_DMA_UNROLL = 4


def _dispatch_kernel(dest_ref, h_ref, xs_in_ref, xs_ref, sem):
    del xs_in_ref

    def row_copy(r, k):
        return pltpu.make_async_copy(
            h_ref.at[pl.ds(r, 1)], xs_ref.at[pl.ds(dest_ref[r * TOP_K + k], 1)], sem)

    def start(r, carry):
        for k in range(TOP_K):
            row_copy(r, k).start()
        return carry

    def wait(r, carry):
        for k in range(TOP_K):
            row_copy(r, k).wait()
        return carry

    lax.fori_loop(0, DISPATCH_TOKENS, start, 0, unroll=ROW_DMA_UNROLL)
    lax.fori_loop(0, DISPATCH_TOKENS, wait, 0, unroll=ROW_DMA_UNROLL)


def _dispatch(dest_flat, h2, xs_zero):
    T, D = h2.shape
    n = DISPATCH_TOKENS * TOP_K
    return pl.pallas_call(
        _dispatch_kernel,
        grid=(T // DISPATCH_TOKENS,),
        in_specs=[
            pl.BlockSpec((n,), lambda i: (i,), memory_space=pltpu.SMEM),
            pl.BlockSpec((DISPATCH_TOKENS, D), lambda i: (i, 0)),
            pl.BlockSpec(memory_space=pl.ANY),
        ],
        out_specs=pl.BlockSpec(memory_space=pl.ANY),
        out_shape=jax.ShapeDtypeStruct(xs_zero.shape, xs_zero.dtype),
        scratch_shapes=[pltpu.SemaphoreType.DMA(())],
        input_output_aliases={2: 0},
        compiler_params=pltpu.CompilerParams(
            dimension_semantics=("arbitrary",), has_side_effects=True),
        name="dispatch",
    )(dest_flat, h2, xs_zero)


def _expert_starts(be_ref, b):
    return jnp.logical_or(b == 0, be_ref[b] != be_ref[jnp.maximum(b - 1, 0)])


UP_TILE = 512
DEINTERLEAVE_W = 256


def _moe_up_kernel(be_ref, nu_ref, x_ref, w_ref, bg_ref, bl_ref, act_ref, wg_scr, wl_scr):
    b = pl.program_id(1)
    active = b < nu_ref[0]

    @pl.when(jnp.logical_and(active, _expert_starts(be_ref, b)))
    def _():
        src = lax.broadcasted_iota(jnp.int32, (DEINTERLEAVE_W, DEINTERLEAVE_W), 0)
        dst = lax.broadcasted_iota(jnp.int32, (DEINTERLEAVE_W, DEINTERLEAVE_W), 1)
        half = DEINTERLEAVE_W // 2
        want = jnp.where(dst < half, 2 * dst, 2 * (dst - half) + 1)
        sel = (src == want).astype(BF16)
        for g in range(2 * UP_TILE // DEINTERLEAVE_W):
            chunk = w_ref[:, g * DEINTERLEAVE_W:(g + 1) * DEINTERLEAVE_W].astype(BF16)
            moved = _dot(chunk, sel).astype(BF16)
            wg_scr[:, g * half:(g + 1) * half] = moved[:, :half]
            wl_scr[:, g * half:(g + 1) * half] = moved[:, half:]

    @pl.when(active)
    def _():
        x = x_ref[...].astype(BF16)
        gate = jnp.minimum(_dot(x, wg_scr[...]) + bg_ref[...], SWIGLU_LIMIT)
        lin = jnp.clip(_dot(x, wl_scr[...]) + bl_ref[...], -SWIGLU_LIMIT, SWIGLU_LIMIT)
        act = gate * jax.nn.sigmoid(SWIGLU_ALPHA * gate) * (lin + 1.0)
        act_ref[...] = act.astype(act_ref.dtype)

    @pl.when(jnp.logical_not(active))
    def _():
        act_ref[...] = jnp.zeros_like(act_ref)


def _moe_up(block_expert, n_used, xs, w_gate_up, b_gate, b_lin):
    P, D = xs.shape
    dff = w_gate_up.shape[2] // 2
    tn = UP_TILE
    nb = P // MOE_ROWS

    def blk(b, nu):
        return jnp.minimum(b, nu[0] - 1)

    grid_spec = pltpu.PrefetchScalarGridSpec(
        num_scalar_prefetch=2,
        grid=(dff // tn, nb),
        in_specs=[
            pl.BlockSpec((MOE_ROWS, D), lambda j, b, be, nu: (blk(b, nu), 0)),
            pl.BlockSpec((None, D, 2 * tn), lambda j, b, be, nu: (be[blk(b, nu)], 0, j)),
            pl.BlockSpec((None, 1, tn), lambda j, b, be, nu: (be[blk(b, nu)], 0, j)),
            pl.BlockSpec((None, 1, tn), lambda j, b, be, nu: (be[blk(b, nu)], 0, j)),
        ],
        out_specs=pl.BlockSpec((MOE_ROWS, tn), lambda j, b, be, nu: (b, j)),
        scratch_shapes=[pltpu.VMEM((D, tn), BF16), pltpu.VMEM((D, tn), BF16)],
    )
    return pl.pallas_call(
        _moe_up_kernel,
        grid_spec=grid_spec,
        out_shape=jax.ShapeDtypeStruct((P, dff), BF16),
        compiler_params=_cparams(("arbitrary", "arbitrary")),
        name="moe_up",
    )(block_expert, n_used, xs, w_gate_up, b_gate, b_lin)


def _moe_down_kernel(be_ref, nu_ref, a_ref, wd_ref, bd_ref, y_ref, wd_scr):
    b = pl.program_id(1)
    active = b < nu_ref[0]

    @pl.when(jnp.logical_and(active, _expert_starts(be_ref, b)))
    def _():
        wd_scr[...] = wd_ref[...].astype(BF16)

    @pl.when(active)
    def _():
        y_ref[...] = _dot(a_ref[...], wd_scr[...]) + bd_ref[...]

    @pl.when(jnp.logical_not(active))
    def _():
        y_ref[...] = jnp.zeros_like(y_ref)


def _moe_down(block_expert, n_used, act, w_down, b_down):
    P, dff = act.shape
    D = w_down.shape[2]
    tn = 1024
    nb = P // MOE_ROWS

    def blk(b, nu):
        return jnp.minimum(b, nu[0] - 1)

    grid_spec = pltpu.PrefetchScalarGridSpec(
        num_scalar_prefetch=2,
        grid=(D // tn, nb),
        in_specs=[
            pl.BlockSpec((MOE_ROWS, dff), lambda j, b, be, nu: (blk(b, nu), 0)),
            pl.BlockSpec((None, dff, tn), lambda j, b, be, nu: (be[blk(b, nu)], 0, j)),
            pl.BlockSpec((None, 1, tn), lambda j, b, be, nu: (be[blk(b, nu)], 0, j)),
        ],
        out_specs=pl.BlockSpec((MOE_ROWS, tn), lambda j, b, be, nu: (b, j)),
        scratch_shapes=[pltpu.VMEM((dff, tn), BF16)],
    )
    return pl.pallas_call(
        _moe_down_kernel,
        grid_spec=grid_spec,
        out_shape=jax.ShapeDtypeStruct((P, D), F32),
        compiler_params=_cparams(("arbitrary", "arbitrary")),
        name="moe_down",
    )(block_expert, n_used, act, w_down, b_down)


COMBINE_TOKENS = 256


def _combine_kernel(dest_ref, y_ref, x2_ref, rw_ref, o_ref, buf, sem):
    def row_copy(r, k):
        return pltpu.make_async_copy(
            y_ref.at[pl.ds(dest_ref[r * TOP_K + k], 1)], buf.at[k, pl.ds(r, 1)], sem)

    def start(r, carry):
        for k in range(TOP_K):
            row_copy(r, k).start()
        return carry

    def wait(r, carry):
        for k in range(TOP_K):
            row_copy(r, k).wait()
        return carry

    lax.fori_loop(0, COMBINE_TOKENS, start, 0, unroll=ROW_DMA_UNROLL)
    lax.fori_loop(0, COMBINE_TOKENS, wait, 0, unroll=ROW_DMA_UNROLL)
    acc = x2_ref[...]
    for k in range(TOP_K):
        acc = acc + rw_ref[:, k:k + 1] * buf[k]
    o_ref[...] = acc


def _combine(dest_flat, y, x2, route_w):
    T, D = x2.shape
    tc = COMBINE_TOKENS
    return pl.pallas_call(
        _combine_kernel,
        grid=(T // tc,),
        in_specs=[
            pl.BlockSpec((tc * TOP_K,), lambda i: (i,), memory_space=pltpu.SMEM),
            pl.BlockSpec(memory_space=pl.ANY),
            pl.BlockSpec((tc, D), lambda i: (i, 0)),
            pl.BlockSpec((tc, LANES), lambda i: (i, 0)),
        ],
        out_specs=pl.BlockSpec((tc, D), lambda i: (i, 0)),
        out_shape=jax.ShapeDtypeStruct((T, D), F32),
        scratch_shapes=[pltpu.VMEM((TOP_K, tc, D), F32), pltpu.SemaphoreType.DMA(())],
        compiler_params=_cparams(("arbitrary",)),
        name="combine",
    )(dest_flat, y, x2, route_w)


def _pad_lanes(v, offset):
    return jnp.zeros((LANES,), F32).at[offset:offset + v.shape[0]].set(v.astype(F32))


def _layer(x, norm1_w, w_in, b_fox_f, fox_q_norm_w, fox_k_norm_w, gdn_conv_w, gdn_A_log,
           gdn_dt_bias, gdn_out_norm_w, w_up_fox, w_up_gdn, w_o, norm2_w, w_router, b_router,
           w_gate_up, b_gate_up, w_down, b_down):
    B, S, D = x.shape
    T = B * S
    x2d = x.reshape(T, D)

    o_f = 3 * HEADS_W
    o_gqkv = o_f + N_HEADS
    o_gb = o_gqkv + 4 * HEADS_W
    o_gate = o_gb + 2 * N_HEADS
    w_big = jnp.concatenate(
        [w_in[:, o_gate:], w_in[:, :o_f], w_in[:, o_gqkv:o_gb]], axis=1).astype(BF16)
    w_small = jnp.concatenate(
        [w_in[:, o_f:o_gqkv], w_in[:, o_gb:o_gate],
         jnp.zeros((D, LANES - 3 * N_HEADS), w_in.dtype)], axis=1).astype(BF16)

    proj, small, v_t = _in_proj(x2d, norm1_w.reshape(1, D), w_big, w_small,
                                fox_q_norm_w.reshape(1, HEAD_DIM), fox_k_norm_w.reshape(1, HEAD_DIM))

    gate_params = jnp.zeros((8, LANES), F32)
    gate_params = gate_params.at[0].set(_pad_lanes(b_fox_f, LANE_C) + _pad_lanes(gdn_dt_bias, LANE_GC))
    gate_params = gate_params.at[1].set(_pad_lanes(gdn_A_log, LANE_GC))
    gates = _gates(small, gate_params, S)
    gates_t = gates[:, :32].T
    c_bcast = jnp.broadcast_to(
        (gates_t[LANE_C:LANE_C + N_HEADS] * LOG2E)[:, :, None], (N_HEADS, T, LANES))

    y_fox = _fox(proj, v_t, c_bcast, B, S)

    u, w, qd, at, kdt, gl = _gdn_prep(proj, gdn_conv_w.astype(F32), gates, gates_t, S)
    y_gdn = _gdn_scan(u, w, qd, at, kdt, gl, proj, gdn_out_norm_w.reshape(1, HEAD_DIM), B, S)

    w_router_p = jnp.zeros((D, LANES), F32).at[:, :N_EXPERTS].set(w_router.astype(F32))
    b_router_p = _pad_lanes(b_router, 0).reshape(1, LANES)
    x2, h2, logits = _merge(y_fox, y_gdn, proj, x2d, w_up_fox.astype(BF16), w_up_gdn.astype(BF16),
                            w_o.astype(BF16), norm2_w.reshape(1, D), w_router_p, b_router_p)

    route_i, route_w, counts = _route(logits)
    expert = route_i[:, :TOP_K]
    rank = route_i[:, LANE_RANK:LANE_RANK + TOP_K]
    counts = counts[0, :N_EXPERTS].astype(jnp.int32)
    padded = (counts + MOE_ROWS - 1) // MOE_ROWS * MOE_ROWS
    pad_end = jnp.cumsum(padded)
    pad_start = pad_end - padded
    dest_flat = (pad_start[expert] + rank).reshape(T * TOP_K).astype(jnp.int32)
    n_blocks = -(-T * TOP_K // MOE_ROWS) + N_EXPERTS
    n_used = (pad_end[-1:] // MOE_ROWS).astype(jnp.int32)
    block_start = jnp.arange(n_blocks, dtype=jnp.int32) * MOE_ROWS
    block_expert = jnp.minimum(
        jnp.sum((pad_end[None, :] <= block_start[:, None]).astype(jnp.int32), axis=1),
        N_EXPERTS - 1).astype(jnp.int32)

    xs = _dispatch(dest_flat, h2, jnp.zeros((n_blocks * MOE_ROWS, D), F32))

    dff = w_down.shape[1]
    b_gu = b_gate_up.reshape(N_EXPERTS, 1, dff, 2).astype(F32)
    act = _moe_up(block_expert, n_used, xs, w_gate_up.astype(F32), b_gu[..., 0], b_gu[..., 1])
    y = _moe_down(block_expert, n_used, act, w_down.astype(F32),
                  b_down.reshape(N_EXPERTS, 1, D).astype(F32))

    out = _combine(dest_flat, y, x2, route_w)
    return out.reshape(B, S, D)


def kernel(x, norm1_w, w_in, b_fox_f, fox_q_norm_w, fox_k_norm_w, gdn_conv_w, gdn_A_log, gdn_dt_bias, gdn_out_norm_w, w_up_fox, w_up_gdn, w_o, norm2_w, w_router, b_router, w_gate_up, b_gate_up, w_down, b_down):
    depth = norm1_w.shape[0]
    for l in range(depth):
        x = _layer(x, norm1_w[l], w_in[l], b_fox_f[l], fox_q_norm_w[l], fox_k_norm_w[l],
                   gdn_conv_w[l], gdn_A_log[l], gdn_dt_bias[l], gdn_out_norm_w[l], w_up_fox[l],
                   w_up_gdn[l], w_o[l], norm2_w[l], w_router[l], b_router[l], w_gate_up[l],
                   b_gate_up[l], w_down[l], b_down[l])
    return x
```

```python
import functools

import jax
import jax.numpy as jnp
from jax import lax
from jax.experimental import pallas as pl
from jax.experimental.pallas import tpu as pltpu

F32 = jnp.float32
BF16 = jnp.bfloat16

N_HEADS = 8
HEAD_DIM = 128
HEADS_W = N_HEADS * HEAD_DIM
GDN_CHUNK = 64
CONV_WIDTH = 4
N_EXPERTS = 32
TOP_K = 4
SWIGLU_LIMIT = 7.0
SWIGLU_ALPHA = 1.702
RMS_EPS = 1e-6
LANES = 128
NEG_BIG = -1e30
LOG2E = 1.4426950408889634

VMEM_LIMIT = 56 * 1024 * 1024

MOE_ROWS = 512


def _cparams(sem):
    return pltpu.CompilerParams(dimension_semantics=sem, vmem_limit_bytes=VMEM_LIMIT)


def _nt_dot(a, b):
    return lax.dot_general(a, b, (((1,), (1,)), ((), ())), preferred_element_type=F32)


def _dot(a, b):
    return jnp.dot(a, b, preferred_element_type=F32)


def _dot_exact(a, b):
    return jnp.dot(a, b, preferred_element_type=F32, precision=lax.Precision.HIGHEST)


def _silu(x):
    return x * jax.nn.sigmoid(x)


BLK_GATE_A, BLK_GATE_B = 0, 2
BLK_FQ, BLK_FK, BLK_FV = 4, 5, 6
BLK_GQ, BLK_GK, BLK_GV, BLK_GZ = 7, 8, 9, 10
CHUNK_SHIFT = GDN_CHUNK.bit_length() - 1


def _in_proj_kernel(x_ref, n1_ref, w_ref, ws_ref, qn_ref, kn_ref, proj_ref, small_ref, vt_ref,
                    h_scr):
    j = pl.program_id(1)

    @pl.when(j == 0)
    def _():
        x = x_ref[...]
        ms = jnp.mean(x * x, axis=-1, keepdims=True)
        h = (x * lax.rsqrt(ms + RMS_EPS) * n1_ref[...]).astype(BF16)
        h_scr[...] = h
        small_ref[...] = _dot(h, ws_ref[...])

    acc = _dot(h_scr[...], w_ref[...])

    def head_norm(nw_ref, scale):
        for hh in range(N_HEADS):
            a = acc[:, hh * HEAD_DIM:(hh + 1) * HEAD_DIM]
            ms = jnp.mean(a * a, axis=-1, keepdims=True)
            y = a * lax.rsqrt(ms + RMS_EPS) * nw_ref[...] * scale
            proj_ref[:, hh * HEAD_DIM:(hh + 1) * HEAD_DIM] = y.astype(BF16)

    @pl.when(j == BLK_FQ)
    def _():
        head_norm(qn_ref, HEAD_DIM ** -0.5 * LOG2E)

    @pl.when(j == BLK_FK)
    def _():
        head_norm(kn_ref, 1.0)

    @pl.when(j == BLK_FV)
    def _():
        vt_ref[...] = acc.T.astype(BF16)

    @pl.when(jnp.logical_and(j != BLK_FQ, j != BLK_FK))
    def _():
        proj_ref[...] = acc.astype(BF16)


def _in_proj(x2d, n1, w_big, w_small, qn, kn):
    T, D = x2d.shape
    n_cols = w_big.shape[1]
    tm = min(1024, T)
    tn = HEADS_W
    return pl.pallas_call(
        _in_proj_kernel,
        grid=(T // tm, n_cols // tn),
        in_specs=[
            pl.BlockSpec((tm, D), lambda i, j: (i, 0)),
            pl.BlockSpec((1, D), lambda i, j: (0, 0)),
            pl.BlockSpec((D, tn), lambda i, j: (0, j)),
            pl.BlockSpec((D, LANES), lambda i, j: (0, 0)),
            pl.BlockSpec((1, HEAD_DIM), lambda i, j: (0, 0)),
            pl.BlockSpec((1, HEAD_DIM), lambda i, j: (0, 0)),
        ],
        out_specs=[
            pl.BlockSpec((tm, tn), lambda i, j: (i, j)),
            pl.BlockSpec((tm, LANES), lambda i, j: (i, 0)),
            pl.BlockSpec((HEADS_W, tm), lambda i, j: (0, i)),
        ],
        out_shape=[
            jax.ShapeDtypeStruct((T, n_cols), BF16),
            jax.ShapeDtypeStruct((T, LANES), F32),
            jax.ShapeDtypeStruct((HEADS_W, T), BF16),
        ],
        scratch_shapes=[pltpu.VMEM((tm, D), BF16)],
        compiler_params=_cparams(("arbitrary", "arbitrary")),
        name="in_proj",
    )(x2d, n1, w_big, w_small, qn, kn)


LANE_C, LANE_BETA, LANE_GC = 0, N_HEADS, 2 * N_HEADS


def _gates_kernel(s_ref, p_ref, g_ref, carry, *, tiles_per_seq):
    i = pl.program_id(0)
    tg = s_ref.shape[0]

    @pl.when(i % tiles_per_seq == 0)
    def _():
        carry[...] = jnp.zeros_like(carry)

    z = s_ref[...] + p_ref[0:1, :]
    soft = jnp.log1p(jnp.exp(-jnp.abs(z)))
    log_sig = jnp.minimum(z, 0.0) - soft
    softplus = jnp.maximum(z, 0.0) + soft
    beta = jax.nn.sigmoid(z)
    g = -jnp.exp(p_ref[1:2, :]) * softplus

    row = lax.broadcasted_iota(jnp.int32, (tg, tg), 0)
    col = lax.broadcasted_iota(jnp.int32, (tg, tg), 1)
    tri = row >= col
    tri_chunk = jnp.logical_and(tri, (row >> CHUNK_SHIFT) == (col >> CHUNK_SHIFT))
    c = _dot_exact(tri.astype(F32), log_sig) + carry[...]
    gc = _dot_exact(tri_chunk.astype(F32), g)
    carry[...] = c[tg - 1:tg, :]

    lane = lax.broadcasted_iota(jnp.int32, (tg, LANES), 1)
    g_ref[...] = jnp.where(lane < LANE_BETA, c,
                           jnp.where(lane < LANE_GC, beta,
                                     jnp.where(lane < LANE_GC + N_HEADS, gc, 0.0)))


def _gates(small, params, seq_len):
    T = small.shape[0]
    tg = min(256, seq_len)
    return pl.pallas_call(
        functools.partial(_gates_kernel, tiles_per_seq=seq_len // tg),
        grid=(T // tg,),
        in_specs=[
            pl.BlockSpec((tg, LANES), lambda i: (i, 0)),
            pl.BlockSpec((8, LANES), lambda i: (0, 0)),
        ],
        out_specs=pl.BlockSpec((tg, LANES), lambda i: (i, 0)),
        out_shape=jax.ShapeDtypeStruct((T, LANES), F32),
        scratch_shapes=[pltpu.VMEM((1, LANES), F32)],
        compiler_params=_cparams(("arbitrary",)),
        name="gates",
    )(small, params)


FOX_TILE = 1024


def _fox_kernel(qi_ref, kj_ref, q_ref, k_ref, vt_ref, cb_ref, o_ref, m_scr, l_scr, acc_scr):
    p = pl.program_id(2)
    i = qi_ref[p]
    j = kj_ref[p]
    tq, tk = q_ref.shape[0], k_ref.shape[0]

    @pl.when(j == 0)
    def _():
        m_scr[...] = jnp.full_like(m_scr, NEG_BIG)
        l_scr[...] = jnp.zeros_like(l_scr)
        acc_scr[...] = jnp.zeros_like(acc_scr)

    def step(masked):
        st = _nt_dot(k_ref[...], q_ref[...])
        st = st - jnp.concatenate([cb_ref[...]] * (tq // LANES), axis=1)
        if masked:
            key = lax.broadcasted_iota(jnp.int32, (tk, tq), 0)
            qry = lax.broadcasted_iota(jnp.int32, (tk, tq), 1)
            st = jnp.where(qry >= key, st, -jnp.inf)
        m_old = m_scr[...]
        m_new = jnp.maximum(m_old, jnp.max(st, axis=0, keepdims=True))
        alpha = jnp.exp2(m_old - m_new)
        pt = jnp.exp2(st - m_new)
        l_scr[...] = alpha * l_scr[...] + jnp.sum(pt, axis=0, keepdims=True)
        acc_scr[...] = alpha * acc_scr[...] + _dot(vt_ref[...], pt.astype(BF16))
        m_scr[...] = m_new

    @pl.when(j < i)
    def _():
        step(False)

    @pl.when(j == i)
    def _():
        step(True)
        o_ref[...] = (acc_scr[...] / l_scr[...]).T.astype(o_ref.dtype)


def _fox(proj, v_t, c_bcast, batch, seq_len):
    T = proj.shape[0]
    tq = min(FOX_TILE, seq_len)
    nq = seq_len // tq
    pairs = [(i, j) for i in range(nq) for j in range(i + 1)]
    qi = jnp.asarray([p[0] for p in pairs], jnp.int32)
    kj = jnp.asarray([p[1] for p in pairs], jnp.int32)
    q_off, k_off = BLK_FQ * N_HEADS, BLK_FK * N_HEADS
    grid_spec = pltpu.PrefetchScalarGridSpec(
        num_scalar_prefetch=2,
        grid=(batch, N_HEADS, len(pairs)),
        in_specs=[
            pl.BlockSpec((tq, HEAD_DIM), lambda b, h, p, qi, kj: (b * nq + qi[p], q_off + h)),
            pl.BlockSpec((tq, HEAD_DIM), lambda b, h, p, qi, kj: (b * nq + kj[p], k_off + h)),
            pl.BlockSpec((HEAD_DIM, tq), lambda b, h, p, qi, kj: (h, b * nq + kj[p])),
            pl.BlockSpec((None, tq, LANES), lambda b, h, p, qi, kj: (h, b * nq + kj[p], 0)),
        ],
        out_specs=pl.BlockSpec((tq, HEAD_DIM), lambda b, h, p, qi, kj: (b * nq + qi[p], h)),
        scratch_shapes=[
            pltpu.VMEM((1, tq), F32),
            pltpu.VMEM((1, tq), F32),
            pltpu.VMEM((HEAD_DIM, tq), F32),
        ],
    )
    return pl.pallas_call(
        _fox_kernel,
        grid_spec=grid_spec,
        out_shape=jax.ShapeDtypeStruct((T, HEADS_W), BF16),
        compiler_params=_cparams(("arbitrary", "arbitrary", "arbitrary")),
        name="fox",
    )(qi, kj, proj, proj, v_t, c_bcast)


GDN_TILE = 2 * GDN_CHUNK
HALO = 16


def _gdn_prep_kernel(q_ref, k_ref, v_ref, hq_ref, hk_ref, hv_ref, cw_ref, g_ref, gt_ref,
                     u_ref, w_ref, qd_ref, at_ref, kdt_ref, gl_ref, *, tiles_per_seq):
    i = pl.program_id(0)
    R = GDN_TILE
    first = (i % tiles_per_seq) == 0
    heads = range(N_HEADS)

    def hslice(x, hh):
        return x[:, hh * HEAD_DIM:(hh + 1) * HEAD_DIM]

    dst_t = lax.broadcasted_iota(jnp.int32, (R, 2 * R), 0)
    src_t = lax.broadcasted_iota(jnp.int32, (R, 2 * R), 1)
    src_ok = jnp.logical_or(src_t >= HALO, jnp.logical_not(first))
    shifts = [jnp.logical_and(src_t == dst_t + HALO - back, src_ok).astype(BF16)
              for back in range(CONV_WIDTH - 1, 0, -1)]
    zero_rows = jnp.zeros((2 * R - HALO - R, HEADS_W), BF16)

    def conv_silu(grp, m_ref, h_ref):
        main = m_ref[...]
        stacked = jnp.concatenate([h_ref[...], main, zero_rows], axis=0)
        cw = cw_ref[:, grp * HEADS_W:(grp + 1) * HEADS_W]
        acc = main.astype(F32) * cw[CONV_WIDTH - 1:CONV_WIDTH, :]
        for tap in range(CONV_WIDTH - 1):
            acc = acc + _dot(shifts[tap], stacked) * cw[tap:tap + 1, :]
        return _silu(acc)

    q_all = conv_silu(0, q_ref, hq_ref)
    k_all = conv_silu(1, k_ref, hk_ref)
    v_all = conv_silu(2, v_ref, hv_ref)

    def l2n(x):
        return x * lax.rsqrt(jnp.sum(x * x, axis=-1, keepdims=True) + RMS_EPS)

    qn = [l2n(hslice(q_all, hh)) * (HEAD_DIM ** -0.5) for hh in heads]
    kn = [l2n(hslice(k_all, hh)) for hh in heads]
    qb = [x.astype(BF16) for x in qn]
    kb = [x.astype(BF16) for x in kn]
    vb = [hslice(v_all, hh).astype(BF16) for hh in heads]

    row = lax.broadcasted_iota(jnp.int32, (R, R), 0)
    col = lax.broadcasted_iota(jnp.int32, (R, R), 1)
    same = (row >> CHUNK_SHIFT) == (col >> CHUNK_SHIFT)
    incl = jnp.logical_and(same, row >= col)
    strict = jnp.logical_and(same, row > col)
    eye = (row == col).astype(F32)
    top = row < GDN_CHUNK

    gcol = [jnp.broadcast_to(g_ref[:, LANE_GC + hh:LANE_GC + hh + 1], (R, R)) for hh in heads]
    bcol = [jnp.broadcast_to(g_ref[:, LANE_BETA + hh:LANE_BETA + hh + 1], (R, R)) for hh in heads]
    grow = [gt_ref[LANE_GC + hh:LANE_GC + hh + 1, :] for hh in heads]
    brow = [gt_ref[LANE_BETA + hh:LANE_BETA + hh + 1, :] for hh in heads]

    decay = [jnp.where(incl, jnp.exp(jnp.minimum(gcol[hh] - grow[hh], 0.0)), 0.0) for hh in heads]
    kk = [_nt_dot(kb[hh], kb[hh]) for hh in heads]
    qk = [_nt_dot(qb[hh], kb[hh]) for hh in heads]
    lmat = [jnp.where(strict, kk[hh] * bcol[hh] * decay[hh], 0.0) for hh in heads]

    inv = [eye - lm for lm in lmat]
    power = lmat
    for _ in range(GDN_CHUNK.bit_length() - 2):
        pb = [pw.astype(BF16) for pw in power]
        power = [_dot(x, x) for x in pb]
        inv = [iv + _dot(iv.astype(BF16), pw.astype(BF16)) for iv, pw in zip(inv, power)]

    sub = lax.broadcasted_iota(jnp.int32, (8, R), 0)
    for hh in heads:
        cs = hh * HEAD_DIM
        u = _dot((inv[hh] * brow[hh]).astype(BF16), vb[hh])
        w = _dot((inv[hh] * (brow[hh] * jnp.exp(grow[hh]))).astype(BF16), kb[hh])

        g_last0 = gcol[hh][GDN_CHUNK - 1:GDN_CHUNK, :]
        g_last1 = gcol[hh][R - 1:R, :]
        g_last = jnp.where(top, g_last0, g_last1)
        q_dec = qn[hh] * jnp.exp(gcol[hh])
        k_dec = kn[hh] * jnp.exp(g_last - gcol[hh])

        attn = qk[hh] * decay[hh]
        attn_own = jnp.where(top, attn, pltpu.roll(attn, GDN_CHUNK, 1))
        attn_own = jnp.where(col < GDN_CHUNK, attn_own, 0.0)

        u_ref[:, cs:cs + HEAD_DIM] = u.astype(BF16)
        w_ref[:, cs:cs + HEAD_DIM] = w.astype(BF16)
        qd_ref[:, cs:cs + HEAD_DIM] = q_dec.astype(BF16)
        at_ref[:, cs:cs + HEAD_DIM] = attn_own.astype(BF16)
        kdt_ref[cs:cs + HEAD_DIM, :] = k_dec.T.astype(BF16)
        gl_ref[:, cs:cs + HEAD_DIM] = jnp.where(
            sub == 0, jnp.exp(g_last0), jnp.where(sub == 1, jnp.exp(g_last1), 0.0))


def _gdn_prep(proj, conv_w, gates, gates_t, seq_len):
    T = proj.shape[0]
    R = GDN_TILE
    n_tiles = T // R
    q_blk, k_blk, v_blk = BLK_GQ, BLK_GK, BLK_GV
    per_tile = R // HALO

    def main_spec(cb):
        return pl.BlockSpec((R, HEADS_W), lambda i: (i, cb))

    def halo_spec(cb):
        return pl.BlockSpec((HALO, HEADS_W), lambda i: (jnp.maximum(i * per_tile - 1, 0), cb))

    wide = pl.BlockSpec((R, HEADS_W), lambda i: (i, 0))
    return pl.pallas_call(
        functools.partial(_gdn_prep_kernel, tiles_per_seq=seq_len // R),
        grid=(n_tiles,),
        in_specs=[
            main_spec(q_blk), main_spec(k_blk), main_spec(v_blk),
            halo_spec(q_blk), halo_spec(k_blk), halo_spec(v_blk),
            pl.BlockSpec((CONV_WIDTH, 3 * HEADS_W), lambda i: (0, 0)),
            pl.BlockSpec((R, LANES), lambda i: (i, 0)),
            pl.BlockSpec((32, R), lambda i: (0, i)),
        ],
        out_specs=[
            wide, wide, wide, wide,
            pl.BlockSpec((HEADS_W, R), lambda i: (0, i)),
            pl.BlockSpec((8, HEADS_W), lambda i: (i, 0)),
        ],
        out_shape=[
            jax.ShapeDtypeStruct((T, HEADS_W), BF16),
            jax.ShapeDtypeStruct((T, HEADS_W), BF16),
            jax.ShapeDtypeStruct((T, HEADS_W), BF16),
            jax.ShapeDtypeStruct((T, HEADS_W), BF16),
            jax.ShapeDtypeStruct((HEADS_W, T), BF16),
            jax.ShapeDtypeStruct((n_tiles * 8, HEADS_W), F32),
        ],
        compiler_params=_cparams(("arbitrary",)),
        name="gdn_prep",
    )(proj, proj, proj, proj, proj, proj, conv_w, gates, gates_t)


def _gdn_scan_kernel(u_ref, w_ref, qd_ref, at_ref, kdt_ref, gl_ref, z_ref, nw_ref, y_ref, state):
    s = pl.program_id(1)
    rows = u_ref.shape[0]

    @pl.when(s == 0)
    def _():
        state[...] = jnp.zeros_like(state)

    zeros_half = jnp.zeros((GDN_CHUNK, HEAD_DIM), BF16)
    heads = range(N_HEADS)

    def blk(ref, r0, hh):
        return ref[r0:r0 + GDN_CHUNK, hh * HEAD_DIM:(hh + 1) * HEAD_DIM]

    st = [state[hh] for hh in heads]
    for c in range(rows // GDN_CHUNK):
        r0 = c * GDN_CHUNK
        tile, half = divmod(c, 2)
        t0 = tile * GDN_TILE
        sb = [x.astype(BF16) for x in st]
        wq = [_dot(jnp.concatenate([blk(w_ref, r0, hh), blk(qd_ref, r0, hh)], axis=0), sb[hh])
              for hh in heads]
        v_new = [(blk(u_ref, r0, hh).astype(F32) - wq[hh][:GDN_CHUNK]).astype(BF16) for hh in heads]
        v_lo = [jnp.concatenate([x, zeros_half], axis=0) for x in v_new]
        v_own = v_lo if half == 0 else [jnp.concatenate([zeros_half, x], axis=0) for x in v_new]
        o = [wq[hh][GDN_CHUNK:] + _dot(blk(at_ref, r0, hh), v_lo[hh]) for hh in heads]
        st = [st[hh] * gl_ref[tile * 8 + half:tile * 8 + half + 1, hh * HEAD_DIM:(hh + 1) * HEAD_DIM]
              + _dot(kdt_ref[hh * HEAD_DIM:(hh + 1) * HEAD_DIM, t0:t0 + GDN_TILE], v_own[hh])
              for hh in heads]
        for hh in heads:
            ms = jnp.mean(o[hh] * o[hh], axis=-1, keepdims=True)
            z = blk(z_ref, r0, hh).astype(F32)
            y = o[hh] * lax.rsqrt(ms + RMS_EPS) * nw_ref[...] * _silu(z)
            y_ref[r0:r0 + GDN_CHUNK, hh * HEAD_DIM:(hh + 1) * HEAD_DIM] = y.astype(y_ref.dtype)
    for hh in heads:
        state[hh] = st[hh]


def _gdn_scan(u, w, qd, at, kdt, gl, proj, out_norm_w, batch, seq_len):
    T = u.shape[0]
    rows = min(256, seq_len)
    ns = seq_len // rows
    z_blk = BLK_GZ
    wide = pl.BlockSpec((rows, HEADS_W), lambda b, s: (b * ns + s, 0))
    gl_rows = rows // GDN_TILE * 8
    return pl.pallas_call(
        _gdn_scan_kernel,
        grid=(batch, ns),
        in_specs=[
            wide, wide, wide, wide,
            pl.BlockSpec((HEADS_W, rows), lambda b, s: (0, b * ns + s)),
            pl.BlockSpec((gl_rows, HEADS_W), lambda b, s: (b * ns + s, 0)),
            pl.BlockSpec((rows, HEADS_W), lambda b, s: (b * ns + s, z_blk)),
            pl.BlockSpec((1, HEAD_DIM), lambda b, s: (0, 0)),
        ],
        out_specs=wide,
        out_shape=jax.ShapeDtypeStruct((T, HEADS_W), BF16),
        scratch_shapes=[pltpu.VMEM((N_HEADS, HEAD_DIM, HEAD_DIM), F32)],
        compiler_params=_cparams(("arbitrary", "arbitrary")),
        name="gdn_scan",
    )(u, w, qd, at, kdt, gl, proj, out_norm_w)


def _merge_kernel(yf_ref, yg_ref, ga_ref, gb_ref, x_ref, wf_ref, wg_ref, wo_ref, n2_ref,
                  wr_ref, br_ref, x2_ref, h2_ref, lg_ref):
    a = _dot(yf_ref[...], wf_ref[...])
    b = _dot(yg_ref[...], wg_ref[...])
    merged = (jax.nn.sigmoid(ga_ref[...].astype(F32)) * a
              + jax.nn.sigmoid(gb_ref[...].astype(F32)) * b)
    x2 = x_ref[...] + _dot(merged.astype(BF16), wo_ref[...])
    x2_ref[...] = x2
    ms = jnp.mean(x2 * x2, axis=-1, keepdims=True)
    h2 = x2 * lax.rsqrt(ms + RMS_EPS) * n2_ref[...]
    h_hi = h2.astype(BF16)
    h_hi32 = h_hi.astype(F32)
    h_lo = (h2 - h_hi32).astype(BF16)

    hw = _dot(h_hi, wr_ref[...])
    lg_ref[...] = (hw[:, :LANES] + hw[:, LANES:] + _dot(h_lo, wr_ref[:, :LANES])) + br_ref[...]

    half = h2.shape[1] // 2
    bits = pltpu.bitcast(h_hi32, jnp.uint32)
    h2_ref[...] = (bits[:, :half] & jnp.uint32(0xFFFF0000)) | (bits[:, half:] >> 16)


def _merge(y_fox, y_gdn, proj, x2d, wf, wg, wo, n2, wr, br):
    T, D = x2d.shape
    tm = min(256, T)
    ga_blk, gb_blk = BLK_GATE_A * HEADS_W // D, BLK_GATE_B * HEADS_W // D
    const = lambda shape: pl.BlockSpec(shape, lambda i: (0, 0), pipeline_mode=pl.Buffered(1))
    return pl.pallas_call(
        _merge_kernel,
        grid=(T // tm,),
        in_specs=[
            pl.BlockSpec((tm, HEADS_W), lambda i: (i, 0)),
            pl.BlockSpec((tm, HEADS_W), lambda i: (i, 0)),
            pl.BlockSpec((tm, D), lambda i: (i, ga_blk)),
            pl.BlockSpec((tm, D), lambda i: (i, gb_blk)),
            pl.BlockSpec((tm, D), lambda i: (i, 0)),
            const((HEADS_W, D)), const((HEADS_W, D)), const((D, D)), const((1, D)),
            const((D, 2 * LANES)), const((1, LANES)),
        ],
        out_specs=[
            pl.BlockSpec((tm, D), lambda i: (i, 0)),
            pl.BlockSpec((tm, D // 2), lambda i: (i, 0)),
            pl.BlockSpec((tm, LANES), lambda i: (i, 0)),
        ],
        out_shape=[
            jax.ShapeDtypeStruct((T, D), F32),
            jax.ShapeDtypeStruct((T, D // 2), jnp.uint32),
            jax.ShapeDtypeStruct((T, LANES), F32),
        ],
        compiler_params=_cparams(("arbitrary",)),
        name="merge",
    )(y_fox, y_gdn, proj, proj, x2d, wf, wg, wo, n2, wr, br)


LANE_RANK = TOP_K


def _route_kernel(lg_ref, ri_ref, rw_ref, cnt_ref, carry):
    i = pl.program_id(0)
    tm = lg_ref.shape[0]

    @pl.when(i == 0)
    def _():
        carry[...] = jnp.zeros_like(carry)

    lane = lax.broadcasted_iota(jnp.int32, (tm, LANES), 1)
    lane_f = lane.astype(F32)
    v = jnp.where(lane < N_EXPERTS, lg_ref[...], -jnp.inf)
    vals, idxs, hots = [], [], []
    for _ in range(TOP_K):
        m = jnp.max(v, axis=1, keepdims=True)
        idx = jnp.min(jnp.where(v == m, lane_f, float(LANES)), axis=1, keepdims=True)
        hot = lane_f == idx
        vals.append(m)
        idxs.append(idx)
        hots.append(hot)
        v = jnp.where(hot, -jnp.inf, v)

    exps = [jnp.exp(val - vals[0]) for val in vals]
    den = exps[0] + exps[1] + exps[2] + exps[3]

    multi_hot = jnp.zeros((tm, LANES), F32)
    for hot in hots:
        multi_hot = multi_hot + hot.astype(F32)
    row = lax.broadcasted_iota(jnp.int32, (tm, tm), 0)
    col = lax.broadcasted_iota(jnp.int32, (tm, tm), 1)
    before = _dot((row > col).astype(BF16), multi_hot.astype(BF16)) + carry[0:1, :]
    total = before[tm - 1:tm, :] + multi_hot[tm - 1:tm, :]
    carry[...] = jnp.broadcast_to(total, carry.shape)
    cnt_ref[...] = jnp.broadcast_to(total, cnt_ref.shape)

    out_i = jnp.zeros((tm, LANES), jnp.int32)
    out_w = jnp.zeros((tm, LANES), F32)
    for k in range(TOP_K):
        rank = jnp.sum(jnp.where(hots[k], before, 0.0), axis=1, keepdims=True).astype(jnp.int32)
        out_i = jnp.where(lane == k, idxs[k].astype(jnp.int32), out_i)
        out_i = jnp.where(lane == LANE_RANK + k, rank, out_i)
        out_w = jnp.where(lane == k, exps[k] / den, out_w)
    ri_ref[...] = out_i
    rw_ref[...] = out_w


def _route(logits):
    T = logits.shape[0]
    tm = min(256, T)
    return pl.pallas_call(
        _route_kernel,
        grid=(T // tm,),
        in_specs=[pl.BlockSpec((tm, LANES), lambda i: (i, 0))],
        out_specs=[
            pl.BlockSpec((tm, LANES), lambda i: (i, 0)),
            pl.BlockSpec((tm, LANES), lambda i: (i, 0)),
            pl.BlockSpec((8, LANES), lambda i: (0, 0)),
        ],
        out_shape=[
            jax.ShapeDtypeStruct((T, LANES), jnp.int32),
            jax.ShapeDtypeStruct((T, LANES), F32),
            jax.ShapeDtypeStruct((8, LANES), F32),
        ],
        scratch_shapes=[pltpu.VMEM((8, LANES), F32)],
        compiler_params=_cparams(("arbitrary",)),
        name="route",
    )(logits)


DISPATCH_TOKENS = 256
ROW_DMA_UNROLL = 4


def _dispatch_kernel(dest_ref, h_ref, xs_in_ref, xs_ref, sem):
    del xs_in_ref

    def row_copy(r, k):
        return pltpu.make_async_copy(
            h_ref.at[pl.ds(r, 1)], xs_ref.at[pl.ds(dest_ref[r * TOP_K + k], 1)], sem)

    def start(r, carry):
        for k in range(TOP_K):
            row_copy(r, k).start()
        return carry

    def wait(r, carry):
        for k in range(TOP_K):
            row_copy(r, k).wait()
        return carry

    lax.fori_loop(0, DISPATCH_TOKENS, start, 0, unroll=ROW_DMA_UNROLL)
    lax.fori_loop(0, DISPATCH_TOKENS, wait, 0, unroll=ROW_DMA_UNROLL)


def _dispatch(dest_flat, h2, xs_zero):
    T, D = h2.shape
    n = DISPATCH_TOKENS * TOP_K
    return pl.pallas_call(
        _dispatch_kernel,
        grid=(T // DISPATCH_TOKENS,),
        in_specs=[
            pl.BlockSpec((n,), lambda i: (i,), memory_space=pltpu.SMEM),
            pl.BlockSpec((DISPATCH_TOKENS, D), lambda i: (i, 0)),
            pl.BlockSpec(memory_space=pl.ANY),
        ],
        out_specs=pl.BlockSpec(memory_space=pl.ANY),
        out_shape=jax.ShapeDtypeStruct(xs_zero.shape, xs_zero.dtype),
        scratch_shapes=[pltpu.SemaphoreType.DMA(())],
        input_output_aliases={2: 0},
        compiler_params=pltpu.CompilerParams(
            dimension_semantics=("arbitrary",), has_side_effects=True),
        name="dispatch",
    )(dest_flat, h2, xs_zero)


def _expert_starts(be_ref, b):
    return jnp.logical_or(b == 0, be_ref[b] != be_ref[jnp.maximum(b - 1, 0)])


UP_TILE = 512
DEINTERLEAVE_W = 256


def _moe_up_kernel(be_ref, nu_ref, x_ref, w_ref, bg_ref, bl_ref, act_ref, wg_scr, wl_scr):
    b = pl.program_id(1)
    active = b < nu_ref[0]

    @pl.when(jnp.logical_and(active, _expert_starts(be_ref, b)))
    def _():
        src = lax.broadcasted_iota(jnp.int32, (DEINTERLEAVE_W, DEINTERLEAVE_W), 0)
        dst = lax.broadcasted_iota(jnp.int32, (DEINTERLEAVE_W, DEINTERLEAVE_W), 1)
        half = DEINTERLEAVE_W // 2
        want = jnp.where(dst < half, 2 * dst, 2 * (dst - half) + 1)
        sel = (src == want).astype(BF16)
        for g in range(2 * UP_TILE // DEINTERLEAVE_W):
            chunk = w_ref[:, g * DEINTERLEAVE_W:(g + 1) * DEINTERLEAVE_W].astype(BF16)
            moved = _dot(chunk, sel).astype(BF16)
            wg_scr[:, g * half:(g + 1) * half] = moved[:, :half]
            wl_scr[:, g * half:(g + 1) * half] = moved[:, half:]

    @pl.when(active)
    def _():
        words = x_ref[...]
        x_hi = pltpu.bitcast(words & jnp.uint32(0xFFFF0000), F32).astype(BF16)
        x_lo = pltpu.bitcast(words << 16, F32).astype(BF16)
        x = jnp.concatenate([x_hi, x_lo], axis=1)
        gate = jnp.minimum(_dot(x, wg_scr[...]) + bg_ref[...], SWIGLU_LIMIT)
        lin = jnp.clip(_dot(x, wl_scr[...]) + bl_ref[...], -SWIGLU_LIMIT, SWIGLU_LIMIT)
        act = gate * jax.nn.sigmoid(SWIGLU_ALPHA * gate) * (lin + 1.0)
        act_ref[...] = act.astype(act_ref.dtype)

    @pl.when(jnp.logical_not(active))
    def _():
        act_ref[...] = jnp.zeros_like(act_ref)


def _moe_up(block_expert, n_used, xs, w_gate_up, b_gate, b_lin):
    P = xs.shape[0]
    D = w_gate_up.shape[1]
    dff = w_gate_up.shape[2] // 2
    tn = UP_TILE
    nb = P // MOE_ROWS

    def blk(b, nu):
        return jnp.minimum(b, nu[0] - 1)

    grid_spec = pltpu.PrefetchScalarGridSpec(
        num_scalar_prefetch=2,
        grid=(dff // tn, nb),
        in_specs=[
            pl.BlockSpec((MOE_ROWS, D // 2), lambda j, b, be, nu: (blk(b, nu), 0)),
            pl.BlockSpec((None, D, 2 * tn), lambda j, b, be, nu: (be[blk(b, nu)], 0, j)),
            pl.BlockSpec((None, 1, tn), lambda j, b, be, nu: (be[blk(b, nu)], 0, j)),
            pl.BlockSpec((None, 1, tn), lambda j, b, be, nu: (be[blk(b, nu)], 0, j)),
        ],
        out_specs=pl.BlockSpec((MOE_ROWS, tn), lambda j, b, be, nu: (b, j)),
        scratch_shapes=[pltpu.VMEM((D, tn), BF16), pltpu.VMEM((D, tn), BF16)],
    )
    return pl.pallas_call(
        _moe_up_kernel,
        grid_spec=grid_spec,
        out_shape=jax.ShapeDtypeStruct((P, dff), BF16),
        compiler_params=_cparams(("arbitrary", "arbitrary")),
        name="moe_up",
    )(block_expert, n_used, xs, w_gate_up, b_gate, b_lin)


def _moe_down_kernel(be_ref, nu_ref, a_ref, wd_ref, bd_ref, y_ref, wd_scr):
    b = pl.program_id(1)
    active = b < nu_ref[0]

    @pl.when(jnp.logical_and(active, _expert_starts(be_ref, b)))
    def _():
        wd_scr[...] = wd_ref[...].astype(BF16)

    @pl.when(active)
    def _():
        y_ref[...] = _dot(a_ref[...], wd_scr[...]) + bd_ref[...]

    @pl.when(jnp.logical_not(active))
    def _():
        y_ref[...] = jnp.zeros_like(y_ref)


def _moe_down(block_expert, n_used, act, w_down, b_down):
    P, dff = act.shape
    D = w_down.shape[2]
    tn = 1024
    nb = P // MOE_ROWS

    def blk(b, nu):
        return jnp.minimum(b, nu[0] - 1)

    grid_spec = pltpu.PrefetchScalarGridSpec(
        num_scalar_prefetch=2,
        grid=(D // tn, nb),
        in_specs=[
            pl.BlockSpec((MOE_ROWS, dff), lambda j, b, be, nu: (blk(b, nu), 0)),
            pl.BlockSpec((None, dff, tn), lambda j, b, be, nu: (be[blk(b, nu)], 0, j)),
            pl.BlockSpec((None, 1, tn), lambda j, b, be, nu: (be[blk(b, nu)], 0, j)),
        ],
        out_specs=pl.BlockSpec((MOE_ROWS, tn), lambda j, b, be, nu: (b, j)),
        scratch_shapes=[pltpu.VMEM((dff, tn), BF16)],
    )
    return pl.pallas_call(
        _moe_down_kernel,
        grid_spec=grid_spec,
        out_shape=jax.ShapeDtypeStruct((P, D), F32),
        compiler_params=_cparams(("arbitrary", "arbitrary")),
        name="moe_down",
    )(block_expert, n_used, act, w_down, b_down)


COMBINE_TOKENS = 256


def _combine_kernel(dest_ref, y_ref, x2_ref, rw_ref, o_ref, buf, sem):
    def row_copy(r, k):
        return pltpu.make_async_copy(
            y_ref.at[pl.ds(dest_ref[r * TOP_K + k], 1)], buf.at[k, pl.ds(r, 1)], sem)

    def start(r, carry):
        for k in range(TOP_K):
            row_copy(r, k).start()
        return carry

    def wait(r, carry):
        for k in range(TOP_K):
            row_copy(r, k).wait()
        return carry

    lax.fori_loop(0, COMBINE_TOKENS, start, 0, unroll=ROW_DMA_UNROLL)
    lax.fori_loop(0, COMBINE_TOKENS, wait, 0, unroll=ROW_DMA_UNROLL)
    acc = x2_ref[...]
    for k in range(TOP_K):
        acc = acc + rw_ref[:, k:k + 1] * buf[k]
    o_ref[...] = acc


def _combine(dest_flat, y, x2, route_w):
    T, D = x2.shape
    tc = COMBINE_TOKENS
    return pl.pallas_call(
        _combine_kernel,
        grid=(T // tc,),
        in_specs=[
            pl.BlockSpec((tc * TOP_K,), lambda i: (i,), memory_space=pltpu.SMEM),
            pl.BlockSpec(memory_space=pl.ANY),
            pl.BlockSpec((tc, D), lambda i: (i, 0)),
            pl.BlockSpec((tc, LANES), lambda i: (i, 0)),
        ],
        out_specs=pl.BlockSpec((tc, D), lambda i: (i, 0)),
        out_shape=jax.ShapeDtypeStruct((T, D), F32),
        scratch_shapes=[pltpu.VMEM((TOP_K, tc, D), F32), pltpu.SemaphoreType.DMA(())],
        compiler_params=_cparams(("arbitrary",)),
        name="combine",
    )(dest_flat, y, x2, route_w)


def _pad_lanes(v, offset):
    return jnp.zeros((LANES,), F32).at[offset:offset + v.shape[0]].set(v.astype(F32))


def _layer(x, norm1_w, w_in, b_fox_f, fox_q_norm_w, fox_k_norm_w, gdn_conv_w, gdn_A_log,
           gdn_dt_bias, gdn_out_norm_w, w_up_fox, w_up_gdn, w_o, norm2_w, w_router, b_router,
           w_gate_up, b_gate_up, w_down, b_down):
    B, S, D = x.shape
    T = B * S
    x2d = x.reshape(T, D)

    o_f = 3 * HEADS_W
    o_gqkv = o_f + N_HEADS
    o_gb = o_gqkv + 4 * HEADS_W
    o_gate = o_gb + 2 * N_HEADS
    w_big = jnp.concatenate(
        [w_in[:, o_gate:], w_in[:, :o_f], w_in[:, o_gqkv:o_gb]], axis=1).astype(BF16)
    w_small = jnp.concatenate(
        [w_in[:, o_f:o_gqkv], w_in[:, o_gb:o_gate],
         jnp.zeros((D, LANES - 3 * N_HEADS), w_in.dtype)], axis=1).astype(BF16)

    proj, small, v_t = _in_proj(x2d, norm1_w.reshape(1, D), w_big, w_small,
                                fox_q_norm_w.reshape(1, HEAD_DIM), fox_k_norm_w.reshape(1, HEAD_DIM))

    gate_params = jnp.zeros((8, LANES), F32)
    gate_params = gate_params.at[0].set(_pad_lanes(b_fox_f, LANE_C) + _pad_lanes(gdn_dt_bias, LANE_GC))
    gate_params = gate_params.at[1].set(_pad_lanes(gdn_A_log, LANE_GC))
    gates = _gates(small, gate_params, S)
    gates_t = gates[:, :32].T
    c_bcast = jnp.broadcast_to(
        (gates_t[LANE_C:LANE_C + N_HEADS] * LOG2E)[:, :, None], (N_HEADS, T, LANES))

    y_fox = _fox(proj, v_t, c_bcast, B, S)

    u, w, qd, at, kdt, gl = _gdn_prep(proj, gdn_conv_w.astype(F32), gates, gates_t, S)
    y_gdn = _gdn_scan(u, w, qd, at, kdt, gl, proj, gdn_out_norm_w.reshape(1, HEAD_DIM), B, S)

    w_router_p = jnp.zeros((D, LANES), F32).at[:, :N_EXPERTS].set(w_router.astype(F32))
    w_router_hi = w_router_p.astype(BF16)
    w_router_lo = (w_router_p - w_router_hi.astype(F32)).astype(BF16)
    w_router_split = jnp.concatenate([w_router_hi, w_router_lo], axis=1)
    b_router_p = _pad_lanes(b_router, 0).reshape(1, LANES)
    x2, h2, logits = _merge(y_fox, y_gdn, proj, x2d, w_up_fox.astype(BF16), w_up_gdn.astype(BF16),
                            w_o.astype(BF16), norm2_w.reshape(1, D), w_router_split, b_router_p)

    route_i, route_w, counts = _route(logits)
    expert = route_i[:, :TOP_K]
    rank = route_i[:, LANE_RANK:LANE_RANK + TOP_K]
    counts = counts[0, :N_EXPERTS].astype(jnp.int32)
    padded = (counts + MOE_ROWS - 1) // MOE_ROWS * MOE_ROWS
    pad_end = jnp.cumsum(padded)
    pad_start = pad_end - padded
    dest_flat = (pad_start[expert] + rank).reshape(T * TOP_K).astype(jnp.int32)
    n_blocks = -(-T * TOP_K // MOE_ROWS) + N_EXPERTS
    n_used = (pad_end[-1:] // MOE_ROWS).astype(jnp.int32)
    block_start = jnp.arange(n_blocks, dtype=jnp.int32) * MOE_ROWS
    block_expert = jnp.minimum(
        jnp.sum((pad_end[None, :] <= block_start[:, None]).astype(jnp.int32), axis=1),
        N_EXPERTS - 1).astype(jnp.int32)

    xs = _dispatch(dest_flat, h2, jnp.zeros((n_blocks * MOE_ROWS, h2.shape[1]), h2.dtype))

    dff = w_down.shape[1]
    b_gu = b_gate_up.reshape(N_EXPERTS, 1, dff, 2).astype(F32)
    act = _moe_up(block_expert, n_used, xs, w_gate_up.astype(F32), b_gu[..., 0], b_gu[..., 1])
    y = _moe_down(block_expert, n_used, act, w_down.astype(F32),
                  b_down.reshape(N_EXPERTS, 1, D).astype(F32))

    out = _combine(dest_flat, y, x2, route_w)
    return out.reshape(B, S, D)


def kernel(x, norm1_w, w_in, b_fox_f, fox_q_norm_w, fox_k_norm_w, gdn_conv_w, gdn_A_log, gdn_dt_bias, gdn_out_norm_w, w_up_fox, w_up_gdn, w_o, norm2_w, w_router, b_router, w_gate_up, b_gate_up, w_down, b_down):
    depth = norm1_w.shape[0]
    for l in range(depth):
        x = _layer(x, norm1_w[l], w_in[l], b_fox_f[l], fox_q_norm_w[l], fox_k_norm_w[l],
                   gdn_conv_w[l], gdn_A_log[l], gdn_dt_bias[l], gdn_out_norm_w[l], w_up_fox[l],
                   w_up_gdn[l], w_o[l], norm2_w[l], w_router[l], b_router[l], w_gate_up[l],
                   b_gate_up[l], w_down[l], b_down[l])
    return x
```

```python
import functools

import jax
import jax.numpy as jnp
from jax import lax
from jax.experimental import pallas as pl
from jax.experimental.pallas import tpu as pltpu

F32 = jnp.float32
BF16 = jnp.bfloat16

N_HEADS = 8
HEAD_DIM = 128
HEADS_W = N_HEADS * HEAD_DIM
GDN_CHUNK = 64
CONV_WIDTH = 4
N_EXPERTS = 32
TOP_K = 4
SWIGLU_LIMIT = 7.0
SWIGLU_ALPHA = 1.702
RMS_EPS = 1e-6
LANES = 128
NEG_BIG = -1e30
LOG2E = 1.4426950408889634

VMEM_LIMIT = 56 * 1024 * 1024

MOE_ROWS = 512


def _cparams(sem):
    return pltpu.CompilerParams(dimension_semantics=sem, vmem_limit_bytes=VMEM_LIMIT)


def _nt_dot(a, b):
    return lax.dot_general(a, b, (((1,), (1,)), ((), ())), preferred_element_type=F32)


def _dot(a, b):
    return jnp.dot(a, b, preferred_element_type=F32)


def _dot_exact(a, b):
    return jnp.dot(a, b, preferred_element_type=F32, precision=lax.Precision.HIGHEST)


def _silu(x):
    return x * jax.nn.sigmoid(x)


BLK_GATE_A, BLK_GATE_B = 0, 2
BLK_FQ, BLK_FK, BLK_FV = 4, 5, 6
BLK_GQ, BLK_GK, BLK_GV, BLK_GZ = 7, 8, 9, 10
CHUNK_SHIFT = GDN_CHUNK.bit_length() - 1


def _in_proj_kernel(x_ref, n1_ref, w_ref, ws_ref, qn_ref, kn_ref, proj_ref, small_ref, vt_ref,
                    h_scr):
    j = pl.program_id(1)

    @pl.when(j == 0)
    def _():
        x = x_ref[...]
        ms = jnp.mean(x * x, axis=-1, keepdims=True)
        h = (x * lax.rsqrt(ms + RMS_EPS) * n1_ref[...]).astype(BF16)
        h_scr[...] = h
        small_ref[...] = _dot(h, ws_ref[...])

    acc = _dot(h_scr[...], w_ref[...])

    def head_norm(nw_ref, scale):
        for hh in range(N_HEADS):
            a = acc[:, hh * HEAD_DIM:(hh + 1) * HEAD_DIM]
            ms = jnp.mean(a * a, axis=-1, keepdims=True)
            y = a * lax.rsqrt(ms + RMS_EPS) * nw_ref[...] * scale
            proj_ref[:, hh * HEAD_DIM:(hh + 1) * HEAD_DIM] = y.astype(BF16)

    @pl.when(j == BLK_FQ)
    def _():
        head_norm(qn_ref, HEAD_DIM ** -0.5 * LOG2E)

    @pl.when(j == BLK_FK)
    def _():
        head_norm(kn_ref, 1.0)

    @pl.when(j == BLK_FV)
    def _():
        vt_ref[...] = acc.T.astype(BF16)

    @pl.when(jnp.logical_and(j != BLK_FQ, j != BLK_FK))
    def _():
        proj_ref[...] = acc.astype(BF16)


def _in_proj(x2d, n1, w_big, w_small, qn, kn):
    T, D = x2d.shape
    n_cols = w_big.shape[1]
    tm = min(1024, T)
    tn = HEADS_W
    return pl.pallas_call(
        _in_proj_kernel,
        grid=(T // tm, n_cols // tn),
        in_specs=[
            pl.BlockSpec((tm, D), lambda i, j: (i, 0)),
            pl.BlockSpec((1, D), lambda i, j: (0, 0)),
            pl.BlockSpec((D, tn), lambda i, j: (0, j)),
            pl.BlockSpec((D, LANES), lambda i, j: (0, 0)),
            pl.BlockSpec((1, HEAD_DIM), lambda i, j: (0, 0)),
            pl.BlockSpec((1, HEAD_DIM), lambda i, j: (0, 0)),
        ],
        out_specs=[
            pl.BlockSpec((tm, tn), lambda i, j: (i, j)),
            pl.BlockSpec((tm, LANES), lambda i, j: (i, 0)),
            pl.BlockSpec((HEADS_W, tm), lambda i, j: (0, i)),
        ],
        out_shape=[
            jax.ShapeDtypeStruct((T, n_cols), BF16),
            jax.ShapeDtypeStruct((T, LANES), F32),
            jax.ShapeDtypeStruct((HEADS_W, T), BF16),
        ],
        scratch_shapes=[pltpu.VMEM((tm, D), BF16)],
        compiler_params=_cparams(("arbitrary", "arbitrary")),
        name="in_proj",
    )(x2d, n1, w_big, w_small, qn, kn)


LANE_C, LANE_BETA, LANE_GC = 0, N_HEADS, 2 * N_HEADS


def _gates_kernel(s_ref, p_ref, g_ref, carry, *, tiles_per_seq):
    i = pl.program_id(0)
    tg = s_ref.shape[0]

    @pl.when(i % tiles_per_seq == 0)
    def _():
        carry[...] = jnp.zeros_like(carry)

    z = s_ref[...] + p_ref[0:1, :]
    soft = jnp.log1p(jnp.exp(-jnp.abs(z)))
    log_sig = jnp.minimum(z, 0.0) - soft
    softplus = jnp.maximum(z, 0.0) + soft
    beta = jax.nn.sigmoid(z)
    g = -jnp.exp(p_ref[1:2, :]) * softplus

    row = lax.broadcasted_iota(jnp.int32, (tg, tg), 0)
    col = lax.broadcasted_iota(jnp.int32, (tg, tg), 1)
    tri = row >= col
    tri_chunk = jnp.logical_and(tri, (row >> CHUNK_SHIFT) == (col >> CHUNK_SHIFT))
    c = _dot_exact(tri.astype(F32), log_sig) + carry[...]
    gc = _dot_exact(tri_chunk.astype(F32), g)
    carry[...] = c[tg - 1:tg, :]

    lane = lax.broadcasted_iota(jnp.int32, (tg, LANES), 1)
    g_ref[...] = jnp.where(lane < LANE_BETA, c,
                           jnp.where(lane < LANE_GC, beta,
                                     jnp.where(lane < LANE_GC + N_HEADS, gc, 0.0)))


def _gates(small, params, seq_len):
    T = small.shape[0]
    tg = min(256, seq_len)
    return pl.pallas_call(
        functools.partial(_gates_kernel, tiles_per_seq=seq_len // tg),
        grid=(T // tg,),
        in_specs=[
            pl.BlockSpec((tg, LANES), lambda i: (i, 0)),
            pl.BlockSpec((8, LANES), lambda i: (0, 0)),
        ],
        out_specs=pl.BlockSpec((tg, LANES), lambda i: (i, 0)),
        out_shape=jax.ShapeDtypeStruct((T, LANES), F32),
        scratch_shapes=[pltpu.VMEM((1, LANES), F32)],
        compiler_params=_cparams(("arbitrary",)),
        name="gates",
    )(small, params)


FOX_TILE = 1024


def _fox_kernel(qi_ref, kj_ref, q_ref, k_ref, vt_ref, cb_ref, o_ref, m_scr, l_scr, acc_scr):
    p = pl.program_id(2)
    i = qi_ref[p]
    j = kj_ref[p]
    tq, tk = q_ref.shape[0], k_ref.shape[0]

    @pl.when(j == 0)
    def _():
        m_scr[...] = jnp.full_like(m_scr, NEG_BIG)
        l_scr[...] = jnp.zeros_like(l_scr)
        acc_scr[...] = jnp.zeros_like(acc_scr)

    def step(masked):
        st = _nt_dot(k_ref[...], q_ref[...])
        st = st - jnp.concatenate([cb_ref[...]] * (tq // LANES), axis=1)
        if masked:
            key = lax.broadcasted_iota(jnp.int32, (tk, tq), 0)
            qry = lax.broadcasted_iota(jnp.int32, (tk, tq), 1)
            st = jnp.where(qry >= key, st, -jnp.inf)
        m_old = m_scr[...]
        m_new = jnp.maximum(m_old, jnp.max(st, axis=0, keepdims=True))
        alpha = jnp.exp2(m_old - m_new)
        pt = jnp.exp2(st - m_new)
        l_scr[...] = alpha * l_scr[...] + jnp.sum(pt, axis=0, keepdims=True)
        acc_scr[...] = alpha * acc_scr[...] + _dot(vt_ref[...], pt.astype(BF16))
        m_scr[...] = m_new

    @pl.when(j < i)
    def _():
        step(False)

    @pl.when(j == i)
    def _():
        step(True)
        o_ref[...] = (acc_scr[...] / l_scr[...]).T.astype(o_ref.dtype)


def _fox(proj, v_t, c_bcast, batch, seq_len):
    T = proj.shape[0]
    tq = min(FOX_TILE, seq_len)
    nq = seq_len // tq
    pairs = [(i, j) for i in range(nq) for j in range(i + 1)]
    qi = jnp.asarray([p[0] for p in pairs], jnp.int32)
    kj = jnp.asarray([p[1] for p in pairs], jnp.int32)
    q_off, k_off = BLK_FQ * N_HEADS, BLK_FK * N_HEADS
    grid_spec = pltpu.PrefetchScalarGridSpec(
        num_scalar_prefetch=2,
        grid=(batch, N_HEADS, len(pairs)),
        in_specs=[
            pl.BlockSpec((tq, HEAD_DIM), lambda b, h, p, qi, kj: (b * nq + qi[p], q_off + h)),
            pl.BlockSpec((tq, HEAD_DIM), lambda b, h, p, qi, kj: (b * nq + kj[p], k_off + h)),
            pl.BlockSpec((HEAD_DIM, tq), lambda b, h, p, qi, kj: (h, b * nq + kj[p])),
            pl.BlockSpec((None, tq, LANES), lambda b, h, p, qi, kj: (h, b * nq + kj[p], 0)),
        ],
        out_specs=pl.BlockSpec((tq, HEAD_DIM), lambda b, h, p, qi, kj: (b * nq + qi[p], h)),
        scratch_shapes=[
            pltpu.VMEM((1, tq), F32),
            pltpu.VMEM((1, tq), F32),
            pltpu.VMEM((HEAD_DIM, tq), F32),
        ],
    )
    return pl.pallas_call(
        _fox_kernel,
        grid_spec=grid_spec,
        out_shape=jax.ShapeDtypeStruct((T, HEADS_W), BF16),
        compiler_params=_cparams(("arbitrary", "arbitrary", "arbitrary")),
        name="fox",
    )(qi, kj, proj, proj, v_t, c_bcast)


GDN_TILE = 2 * GDN_CHUNK
HALO = 16


def _gdn_prep_kernel(q_ref, k_ref, v_ref, hq_ref, hk_ref, hv_ref, cw_ref, g_ref, gt_ref,
                     u_ref, w_ref, qd_ref, at_ref, kdt_ref, gl_ref, *, tiles_per_seq):
    i = pl.program_id(0)
    R = GDN_TILE
    first = (i % tiles_per_seq) == 0
    heads = range(N_HEADS)

    def hslice(x, hh):
        return x[:, hh * HEAD_DIM:(hh + 1) * HEAD_DIM]

    dst_t = lax.broadcasted_iota(jnp.int32, (R, 2 * R), 0)
    src_t = lax.broadcasted_iota(jnp.int32, (R, 2 * R), 1)
    src_ok = jnp.logical_or(src_t >= HALO, jnp.logical_not(first))
    shifts = [jnp.logical_and(src_t == dst_t + HALO - back, src_ok).astype(BF16)
              for back in range(CONV_WIDTH - 1, 0, -1)]
    zero_rows = jnp.zeros((2 * R - HALO - R, HEADS_W), BF16)

    def conv_silu(grp, m_ref, h_ref):
        main = m_ref[...]
        stacked = jnp.concatenate([h_ref[...], main, zero_rows], axis=0)
        cw = cw_ref[:, grp * HEADS_W:(grp + 1) * HEADS_W]
        acc = main.astype(F32) * cw[CONV_WIDTH - 1:CONV_WIDTH, :]
        for tap in range(CONV_WIDTH - 1):
            acc = acc + _dot(shifts[tap], stacked) * cw[tap:tap + 1, :]
        return _silu(acc)

    q_all = conv_silu(0, q_ref, hq_ref)
    k_all = conv_silu(1, k_ref, hk_ref)
    v_all = conv_silu(2, v_ref, hv_ref)

    def l2n(x):
        return x * lax.rsqrt(jnp.sum(x * x, axis=-1, keepdims=True) + RMS_EPS)

    qn = [l2n(hslice(q_all, hh)) * (HEAD_DIM ** -0.5) for hh in heads]
    kn = [l2n(hslice(k_all, hh)) for hh in heads]
    qb = [x.astype(BF16) for x in qn]
    kb = [x.astype(BF16) for x in kn]
    vb = [hslice(v_all, hh).astype(BF16) for hh in heads]

    row = lax.broadcasted_iota(jnp.int32, (R, R), 0)
    col = lax.broadcasted_iota(jnp.int32, (R, R), 1)
    same = (row >> CHUNK_SHIFT) == (col >> CHUNK_SHIFT)
    incl = jnp.logical_and(same, row >= col)
    strict = jnp.logical_and(same, row > col)
    eye = (row == col).astype(F32)
    top = row < GDN_CHUNK

    gcol = [jnp.broadcast_to(g_ref[:, LANE_GC + hh:LANE_GC + hh + 1], (R, R)) for hh in heads]
    bcol = [jnp.broadcast_to(g_ref[:, LANE_BETA + hh:LANE_BETA + hh + 1], (R, R)) for hh in heads]
    grow = [gt_ref[LANE_GC + hh:LANE_GC + hh + 1, :] for hh in heads]
    brow = [gt_ref[LANE_BETA + hh:LANE_BETA + hh + 1, :] for hh in heads]

    decay = [jnp.where(incl, jnp.exp(jnp.minimum(gcol[hh] - grow[hh], 0.0)), 0.0) for hh in heads]
    kk = [_nt_dot(kb[hh], kb[hh]) for hh in heads]
    qk = [_nt_dot(qb[hh], kb[hh]) for hh in heads]
    lmat = [jnp.where(strict, kk[hh] * bcol[hh] * decay[hh], 0.0) for hh in heads]

    inv = [eye - lm for lm in lmat]
    power = lmat
    for _ in range(GDN_CHUNK.bit_length() - 2):
        pb = [pw.astype(BF16) for pw in power]
        power = [_dot(x, x) for x in pb]
        inv = [iv + _dot(iv.astype(BF16), pw.astype(BF16)) for iv, pw in zip(inv, power)]

    sub = lax.broadcasted_iota(jnp.int32, (8, R), 0)
    for hh in heads:
        cs = hh * HEAD_DIM
        u = _dot((inv[hh] * brow[hh]).astype(BF16), vb[hh])
        w = _dot((inv[hh] * (brow[hh] * jnp.exp(grow[hh]))).astype(BF16), kb[hh])

        g_last0 = gcol[hh][GDN_CHUNK - 1:GDN_CHUNK, :]
        g_last1 = gcol[hh][R - 1:R, :]
        g_last = jnp.where(top, g_last0, g_last1)
        q_dec = qn[hh] * jnp.exp(gcol[hh])
        k_dec = kn[hh] * jnp.exp(g_last - gcol[hh])

        attn = qk[hh] * decay[hh]
        attn_own = jnp.where(top, attn, pltpu.roll(attn, GDN_CHUNK, 1))
        attn_own = jnp.where(col < GDN_CHUNK, attn_own, 0.0)

        u_ref[:, cs:cs + HEAD_DIM] = u.astype(BF16)
        w_ref[:, cs:cs + HEAD_DIM] = w.astype(BF16)
        qd_ref[:, cs:cs + HEAD_DIM] = q_dec.astype(BF16)
        at_ref[:, cs:cs + HEAD_DIM] = attn_own.astype(BF16)
        kdt_ref[cs:cs + HEAD_DIM, :] = k_dec.T.astype(BF16)
        gl_ref[:, cs:cs + HEAD_DIM] = jnp.where(
            sub == 0, jnp.exp(g_last0), jnp.where(sub == 1, jnp.exp(g_last1), 0.0))


def _gdn_prep(proj, conv_w, gates, gates_t, seq_len):
    T = proj.shape[0]
    R = GDN_TILE
    n_tiles = T // R
    q_blk, k_blk, v_blk = BLK_GQ, BLK_GK, BLK_GV
    per_tile = R // HALO

    def main_spec(cb):
        return pl.BlockSpec((R, HEADS_W), lambda i: (i, cb))

    def halo_spec(cb):
        return pl.BlockSpec((HALO, HEADS_W), lambda i: (jnp.maximum(i * per_tile - 1, 0), cb))

    wide = pl.BlockSpec((R, HEADS_W), lambda i: (i, 0))
    return pl.pallas_call(
        functools.partial(_gdn_prep_kernel, tiles_per_seq=seq_len // R),
        grid=(n_tiles,),
        in_specs=[
            main_spec(q_blk), main_spec(k_blk), main_spec(v_blk),
            halo_spec(q_blk), halo_spec(k_blk), halo_spec(v_blk),
            pl.BlockSpec((CONV_WIDTH, 3 * HEADS_W), lambda i: (0, 0)),
            pl.BlockSpec((R, LANES), lambda i: (i, 0)),
            pl.BlockSpec((32, R), lambda i: (0, i)),
        ],
        out_specs=[
            wide, wide, wide, wide,
            pl.BlockSpec((HEADS_W, R), lambda i: (0, i)),
            pl.BlockSpec((8, HEADS_W), lambda i: (i, 0)),
        ],
        out_shape=[
            jax.ShapeDtypeStruct((T, HEADS_W), BF16),
            jax.ShapeDtypeStruct((T, HEADS_W), BF16),
            jax.ShapeDtypeStruct((T, HEADS_W), BF16),
            jax.ShapeDtypeStruct((T, HEADS_W), BF16),
            jax.ShapeDtypeStruct((HEADS_W, T), BF16),
            jax.ShapeDtypeStruct((n_tiles * 8, HEADS_W), F32),
        ],
        compiler_params=_cparams(("arbitrary",)),
        name="gdn_prep",
    )(proj, proj, proj, proj, proj, proj, conv_w, gates, gates_t)


def _gdn_scan_kernel(u_ref, w_ref, qd_ref, at_ref, kdt_ref, gl_ref, z_ref, nw_ref, y_ref, state):
    s = pl.program_id(1)
    rows = u_ref.shape[0]

    @pl.when(s == 0)
    def _():
        state[...] = jnp.zeros_like(state)

    zeros_half = jnp.zeros((GDN_CHUNK, HEAD_DIM), BF16)
    heads = range(N_HEADS)

    def blk(ref, r0, hh):
        return ref[r0:r0 + GDN_CHUNK, hh * HEAD_DIM:(hh + 1) * HEAD_DIM]

    st = [state[hh] for hh in heads]
    for c in range(rows // GDN_CHUNK):
        r0 = c * GDN_CHUNK
        tile, half = divmod(c, 2)
        t0 = tile * GDN_TILE
        sb = [x.astype(BF16) for x in st]
        wq = [_dot(jnp.concatenate([blk(w_ref, r0, hh), blk(qd_ref, r0, hh)], axis=0), sb[hh])
              for hh in heads]
        v_new = [(blk(u_ref, r0, hh).astype(F32) - wq[hh][:GDN_CHUNK]).astype(BF16) for hh in heads]
        v_lo = [jnp.concatenate([x, zeros_half], axis=0) for x in v_new]
        v_own = v_lo if half == 0 else [jnp.concatenate([zeros_half, x], axis=0) for x in v_new]
        o = [wq[hh][GDN_CHUNK:] + _dot(blk(at_ref, r0, hh), v_lo[hh]) for hh in heads]
        st = [st[hh] * gl_ref[tile * 8 + half:tile * 8 + half + 1, hh * HEAD_DIM:(hh + 1) * HEAD_DIM]
              + _dot(kdt_ref[hh * HEAD_DIM:(hh + 1) * HEAD_DIM, t0:t0 + GDN_TILE], v_own[hh])
              for hh in heads]
        for hh in heads:
            ms = jnp.mean(o[hh] * o[hh], axis=-1, keepdims=True)
            z = blk(z_ref, r0, hh).astype(F32)
            y = o[hh] * lax.rsqrt(ms + RMS_EPS) * nw_ref[...] * _silu(z)
            y_ref[r0:r0 + GDN_CHUNK, hh * HEAD_DIM:(hh + 1) * HEAD_DIM] = y.astype(y_ref.dtype)
    for hh in heads:
        state[hh] = st[hh]


def _gdn_scan(u, w, qd, at, kdt, gl, proj, out_norm_w, batch, seq_len):
    T = u.shape[0]
    rows = min(256, seq_len)
    ns = seq_len // rows
    z_blk = BLK_GZ
    wide = pl.BlockSpec((rows, HEADS_W), lambda b, s: (b * ns + s, 0))
    gl_rows = rows // GDN_TILE * 8
    return pl.pallas_call(
        _gdn_scan_kernel,
        grid=(batch, ns),
        in_specs=[
            wide, wide, wide, wide,
            pl.BlockSpec((HEADS_W, rows), lambda b, s: (0, b * ns + s)),
            pl.BlockSpec((gl_rows, HEADS_W), lambda b, s: (b * ns + s, 0)),
            pl.BlockSpec((rows, HEADS_W), lambda b, s: (b * ns + s, z_blk)),
            pl.BlockSpec((1, HEAD_DIM), lambda b, s: (0, 0)),
        ],
        out_specs=wide,
        out_shape=jax.ShapeDtypeStruct((T, HEADS_W), BF16),
        scratch_shapes=[pltpu.VMEM((N_HEADS, HEAD_DIM, HEAD_DIM), F32)],
        compiler_params=_cparams(("arbitrary", "arbitrary")),
        name="gdn_scan",
    )(u, w, qd, at, kdt, gl, proj, out_norm_w)


def _merge_kernel(yf_ref, yg_ref, ga_ref, gb_ref, x_ref, wf_ref, wg_ref, wo_ref, n2_ref,
                  wr_ref, br_ref, x2_ref, h2_ref, lg_ref):
    a = _dot(yf_ref[...], wf_ref[...])
    b = _dot(yg_ref[...], wg_ref[...])
    merged = (jax.nn.sigmoid(ga_ref[...].astype(F32)) * a
              + jax.nn.sigmoid(gb_ref[...].astype(F32)) * b)
    x2 = x_ref[...] + _dot(merged.astype(BF16), wo_ref[...])
    x2_ref[...] = x2
    ms = jnp.mean(x2 * x2, axis=-1, keepdims=True)
    h2 = x2 * lax.rsqrt(ms + RMS_EPS) * n2_ref[...]
    h_hi = h2.astype(BF16)
    h_hi32 = h_hi.astype(F32)
    h_lo = (h2 - h_hi32).astype(BF16)

    hw = _dot(h_hi, wr_ref[...])
    lg_ref[...] = (hw[:, :LANES] + hw[:, LANES:] + _dot(h_lo, wr_ref[:, :LANES])) + br_ref[...]

    half = h2.shape[1] // 2
    bits = pltpu.bitcast(h_hi32, jnp.uint32)
    h2_ref[...] = (bits[:, :half] & jnp.uint32(0xFFFF0000)) | (bits[:, half:] >> 16)


def _merge(y_fox, y_gdn, proj, x2d, wf, wg, wo, n2, wr, br):
    T, D = x2d.shape
    tm = min(256, T)
    ga_blk, gb_blk = BLK_GATE_A * HEADS_W // D, BLK_GATE_B * HEADS_W // D
    const = lambda shape: pl.BlockSpec(shape, lambda i: (0, 0), pipeline_mode=pl.Buffered(1))
    return pl.pallas_call(
        _merge_kernel,
        grid=(T // tm,),
        in_specs=[
            pl.BlockSpec((tm, HEADS_W), lambda i: (i, 0)),
            pl.BlockSpec((tm, HEADS_W), lambda i: (i, 0)),
            pl.BlockSpec((tm, D), lambda i: (i, ga_blk)),
            pl.BlockSpec((tm, D), lambda i: (i, gb_blk)),
            pl.BlockSpec((tm, D), lambda i: (i, 0)),
            const((HEADS_W, D)), const((HEADS_W, D)), const((D, D)), const((1, D)),
            const((D, 2 * LANES)), const((1, LANES)),
        ],
        out_specs=[
            pl.BlockSpec((tm, D), lambda i: (i, 0)),
            pl.BlockSpec((tm, D // 2), lambda i: (i, 0)),
            pl.BlockSpec((tm, LANES), lambda i: (i, 0)),
        ],
        out_shape=[
            jax.ShapeDtypeStruct((T, D), F32),
            jax.ShapeDtypeStruct((T, D // 2), jnp.uint32),
            jax.ShapeDtypeStruct((T, LANES), F32),
        ],
        compiler_params=_cparams(("arbitrary",)),
        name="merge",
    )(y_fox, y_gdn, proj, proj, x2d, wf, wg, wo, n2, wr, br)


LANE_RANK = TOP_K


def _route_kernel(lg_ref, ri_ref, rw_ref, cnt_ref, carry):
    i = pl.program_id(0)
    tm = lg_ref.shape[0]

    @pl.when(i == 0)
    def _():
        carry[...] = jnp.zeros_like(carry)

    lane = lax.broadcasted_iota(jnp.int32, (tm, LANES), 1)
    lane_f = lane.astype(F32)
    v = jnp.where(lane < N_EXPERTS, lg_ref[...], -jnp.inf)
    vals, idxs, hots = [], [], []
    for _ in range(TOP_K):
        m = jnp.max(v, axis=1, keepdims=True)
        idx = jnp.min(jnp.where(v == m, lane_f, float(LANES)), axis=1, keepdims=True)
        hot = lane_f == idx
        vals.append(m)
        idxs.append(idx)
        hots.append(hot)
        v = jnp.where(hot, -jnp.inf, v)

    exps = [jnp.exp(val - vals[0]) for val in vals]
    den = exps[0] + exps[1] + exps[2] + exps[3]

    multi_hot = jnp.zeros((tm, LANES), F32)
    for hot in hots:
        multi_hot = multi_hot + hot.astype(F32)
    row = lax.broadcasted_iota(jnp.int32, (tm, tm), 0)
    col = lax.broadcasted_iota(jnp.int32, (tm, tm), 1)
    before = _dot((row > col).astype(BF16), multi_hot.astype(BF16)) + carry[0:1, :]
    total = before[tm - 1:tm, :] + multi_hot[tm - 1:tm, :]
    carry[...] = jnp.broadcast_to(total, carry.shape)
    cnt_ref[...] = jnp.broadcast_to(total, cnt_ref.shape)

    out_i = jnp.zeros((tm, LANES), jnp.int32)
    out_w = jnp.zeros((tm, LANES), F32)
    for k in range(TOP_K):
        rank = jnp.sum(jnp.where(hots[k], before, 0.0), axis=1, keepdims=True).astype(jnp.int32)
        out_i = jnp.where(lane == k, idxs[k].astype(jnp.int32), out_i)
        out_i = jnp.where(lane == LANE_RANK + k, rank, out_i)
        out_w = jnp.where(lane == k, exps[k] / den, out_w)
    ri_ref[...] = out_i
    rw_ref[...] = out_w


def _route(logits):
    T = logits.shape[0]
    tm = min(256, T)
    return pl.pallas_call(
        _route_kernel,
        grid=(T // tm,),
        in_specs=[pl.BlockSpec((tm, LANES), lambda i: (i, 0))],
        out_specs=[
            pl.BlockSpec((tm, LANES), lambda i: (i, 0)),
            pl.BlockSpec((tm, LANES), lambda i: (i, 0)),
            pl.BlockSpec((8, LANES), lambda i: (0, 0)),
        ],
        out_shape=[
            jax.ShapeDtypeStruct((T, LANES), jnp.int32),
            jax.ShapeDtypeStruct((T, LANES), F32),
            jax.ShapeDtypeStruct((8, LANES), F32),
        ],
        scratch_shapes=[pltpu.VMEM((8, LANES), F32)],
        compiler_params=_cparams(("arbitrary",)),
        name="route",
    )(logits)


DISPATCH_TOKENS = 256
ROW_DMA_UNROLL = 4


def _dispatch_kernel(dest_ref, h_ref, xs_in_ref, xs_ref, sem):
    del xs_in_ref

    def row_copy(r, k):
        return pltpu.make_async_copy(
            h_ref.at[pl.ds(r, 1)], xs_ref.at[pl.ds(dest_ref[r * TOP_K + k], 1)], sem)

    def start(r, carry):
        for k in range(TOP_K):
            row_copy(r, k).start()
        return carry

    def wait(r, carry):
        for k in range(TOP_K):
            row_copy(r, k).wait()
        return carry

    lax.fori_loop(0, DISPATCH_TOKENS, start, 0, unroll=ROW_DMA_UNROLL)
    lax.fori_loop(0, DISPATCH_TOKENS, wait, 0, unroll=ROW_DMA_UNROLL)


def _dispatch(dest_flat, h2, xs_zero):
    T, D = h2.shape
    n = DISPATCH_TOKENS * TOP_K
    return pl.pallas_call(
        _dispatch_kernel,
        grid=(T // DISPATCH_TOKENS,),
        in_specs=[
            pl.BlockSpec((n,), lambda i: (i,), memory_space=pltpu.SMEM),
            pl.BlockSpec((DISPATCH_TOKENS, D), lambda i: (i, 0)),
            pl.BlockSpec(memory_space=pl.ANY),
        ],
        out_specs=pl.BlockSpec(memory_space=pl.ANY),
        out_shape=jax.ShapeDtypeStruct(xs_zero.shape, xs_zero.dtype),
        scratch_shapes=[pltpu.SemaphoreType.DMA(())],
        input_output_aliases={2: 0},
        compiler_params=pltpu.CompilerParams(
            dimension_semantics=("arbitrary",), has_side_effects=True),
        name="dispatch",
    )(dest_flat, h2, xs_zero)


def _expert_starts(be_ref, b):
    return jnp.logical_or(b == 0, be_ref[b] != be_ref[jnp.maximum(b - 1, 0)])


UP_TILE = 1024
DOWN_TILE = 1024
DEINTERLEAVE_W = 256


def _weight_tile_stream(be_ref, nxt_ref, w_hbm, stage, sem, width, on_arrival):
    j = pl.program_id(0)
    b = pl.program_id(1)

    def tile_copy(e, jj):
        col = pl.multiple_of(jj * width, width)
        return pltpu.make_async_copy(w_hbm.at[e, :, pl.ds(col, width)], stage, sem)

    @pl.when(jnp.logical_and(j == 0, b == 0))
    def _():
        tile_copy(be_ref[0], 0).start()

    tile_copy(be_ref[b], j).wait()
    on_arrival()
    nb = nxt_ref[b]
    more_experts = nb >= 0
    e_next = be_ref[jnp.where(more_experts, nb, 0)]
    j_next = jnp.where(more_experts, j, j + 1)

    @pl.when(jnp.logical_or(more_experts, j + 1 < pl.num_programs(0)))
    def _():
        tile_copy(e_next, j_next).start()


def _moe_up_kernel(be_ref, nu_ref, nxt_ref, x_ref, w_hbm, bg_ref, bl_ref, act_ref,
                   stage, wg_scr, wl_scr, sem):
    b = pl.program_id(1)
    active = b < nu_ref[0]

    def deinterleave():
        src = lax.broadcasted_iota(jnp.int32, (DEINTERLEAVE_W, DEINTERLEAVE_W), 0)
        dst = lax.broadcasted_iota(jnp.int32, (DEINTERLEAVE_W, DEINTERLEAVE_W), 1)
        half = DEINTERLEAVE_W // 2
        want = jnp.where(dst < half, 2 * dst, 2 * (dst - half) + 1)
        sel = (src == want).astype(BF16)
        for g in range(2 * UP_TILE // DEINTERLEAVE_W):
            chunk = stage[:, g * DEINTERLEAVE_W:(g + 1) * DEINTERLEAVE_W].astype(BF16)
            moved = _dot(chunk, sel).astype(BF16)
            wg_scr[:, g * half:(g + 1) * half] = moved[:, :half]
            wl_scr[:, g * half:(g + 1) * half] = moved[:, half:]

    @pl.when(jnp.logical_and(active, _expert_starts(be_ref, b)))
    def _():
        _weight_tile_stream(be_ref, nxt_ref, w_hbm, stage, sem, 2 * UP_TILE, deinterleave)

    @pl.when(active)
    def _():
        words = x_ref[...]
        x_hi = pltpu.bitcast(words & jnp.uint32(0xFFFF0000), F32).astype(BF16)
        x_lo = pltpu.bitcast(words << 16, F32).astype(BF16)
        x = jnp.concatenate([x_hi, x_lo], axis=1)
        gate = jnp.minimum(_dot(x, wg_scr[...]) + bg_ref[...], SWIGLU_LIMIT)
        lin = jnp.clip(_dot(x, wl_scr[...]) + bl_ref[...], -SWIGLU_LIMIT, SWIGLU_LIMIT)
        act = gate * jax.nn.sigmoid(SWIGLU_ALPHA * gate) * (lin + 1.0)
        act_ref[...] = act.astype(act_ref.dtype)

    @pl.when(jnp.logical_not(active))
    def _():
        act_ref[...] = jnp.zeros_like(act_ref)


def _moe_up(block_expert, n_used, next_start, xs, w_gate_up, b_gate, b_lin):
    P = xs.shape[0]
    D = w_gate_up.shape[1]
    dff = w_gate_up.shape[2] // 2
    tn = UP_TILE
    nb = P // MOE_ROWS

    def blk(b, nu):
        return jnp.minimum(b, nu[0] - 1)

    grid_spec = pltpu.PrefetchScalarGridSpec(
        num_scalar_prefetch=3,
        grid=(dff // tn, nb),
        in_specs=[
            pl.BlockSpec((MOE_ROWS, D // 2), lambda j, b, be, nu, nx: (blk(b, nu), 0)),
            pl.BlockSpec(memory_space=pl.ANY),
            pl.BlockSpec((None, 1, tn), lambda j, b, be, nu, nx: (be[blk(b, nu)], 0, j)),
            pl.BlockSpec((None, 1, tn), lambda j, b, be, nu, nx: (be[blk(b, nu)], 0, j)),
        ],
        out_specs=pl.BlockSpec((MOE_ROWS, tn), lambda j, b, be, nu, nx: (b, j)),
        scratch_shapes=[
            pltpu.VMEM((D, 2 * tn), F32),
            pltpu.VMEM((D, tn), BF16),
            pltpu.VMEM((D, tn), BF16),
            pltpu.SemaphoreType.DMA(()),
        ],
    )
    return pl.pallas_call(
        _moe_up_kernel,
        grid_spec=grid_spec,
        out_shape=jax.ShapeDtypeStruct((P, dff), BF16),
        compiler_params=_cparams(("arbitrary", "arbitrary")),
        name="moe_up",
    )(block_expert, n_used, next_start, xs, w_gate_up, b_gate, b_lin)


def _moe_down_kernel(be_ref, nu_ref, nxt_ref, a_ref, w_hbm, bd_ref, y_ref, stage, wd_scr, sem):
    b = pl.program_id(1)
    active = b < nu_ref[0]

    def to_bf16():
        wd_scr[...] = stage[...].astype(BF16)

    @pl.when(jnp.logical_and(active, _expert_starts(be_ref, b)))
    def _():
        _weight_tile_stream(be_ref, nxt_ref, w_hbm, stage, sem, DOWN_TILE, to_bf16)

    @pl.when(active)
    def _():
        y_ref[...] = _dot(a_ref[...], wd_scr[...]) + bd_ref[...]

    @pl.when(jnp.logical_not(active))
    def _():
        y_ref[...] = jnp.zeros_like(y_ref)


def _moe_down(block_expert, n_used, next_start, act, w_down, b_down):
    P, dff = act.shape
    D = w_down.shape[2]
    tn = DOWN_TILE
    nb = P // MOE_ROWS

    def blk(b, nu):
        return jnp.minimum(b, nu[0] - 1)

    grid_spec = pltpu.PrefetchScalarGridSpec(
        num_scalar_prefetch=3,
        grid=(D // tn, nb),
        in_specs=[
            pl.BlockSpec((MOE_ROWS, dff), lambda j, b, be, nu, nx: (blk(b, nu), 0)),
            pl.BlockSpec(memory_space=pl.ANY),
            pl.BlockSpec((None, 1, tn), lambda j, b, be, nu, nx: (be[blk(b, nu)], 0, j)),
        ],
        out_specs=pl.BlockSpec((MOE_ROWS, tn), lambda j, b, be, nu, nx: (b, j)),
        scratch_shapes=[
            pltpu.VMEM((dff, tn), F32),
            pltpu.VMEM((dff, tn), BF16),
            pltpu.SemaphoreType.DMA(()),
        ],
    )
    return pl.pallas_call(
        _moe_down_kernel,
        grid_spec=grid_spec,
        out_shape=jax.ShapeDtypeStruct((P, D), F32),
        compiler_params=_cparams(("arbitrary", "arbitrary")),
        name="moe_down",
    )(block_expert, n_used, next_start, act, w_down, b_down)


COMBINE_TOKENS = 256


def _combine_kernel(dest_ref, y_ref, x2_ref, rw_ref, o_ref, buf, sem):
    def row_copy(r, k):
        return pltpu.make_async_copy(
            y_ref.at[pl.ds(dest_ref[r * TOP_K + k], 1)], buf.at[k, pl.ds(r, 1)], sem)

    def start(r, carry):
        for k in range(TOP_K):
            row_copy(r, k).start()
        return carry

    def wait(r, carry):
        for k in range(TOP_K):
            row_copy(r, k).wait()
        return carry

    lax.fori_loop(0, COMBINE_TOKENS, start, 0, unroll=ROW_DMA_UNROLL)
    lax.fori_loop(0, COMBINE_TOKENS, wait, 0, unroll=ROW_DMA_UNROLL)
    acc = x2_ref[...]
    for k in range(TOP_K):
        acc = acc + rw_ref[:, k:k + 1] * buf[k]
    o_ref[...] = acc


def _combine(dest_flat, y, x2, route_w):
    T, D = x2.shape
    tc = COMBINE_TOKENS
    return pl.pallas_call(
        _combine_kernel,
        grid=(T // tc,),
        in_specs=[
            pl.BlockSpec((tc * TOP_K,), lambda i: (i,), memory_space=pltpu.SMEM),
            pl.BlockSpec(memory_space=pl.ANY),
            pl.BlockSpec((tc, D), lambda i: (i, 0)),
            pl.BlockSpec((tc, LANES), lambda i: (i, 0)),
        ],
        out_specs=pl.BlockSpec((tc, D), lambda i: (i, 0)),
        out_shape=jax.ShapeDtypeStruct((T, D), F32),
        scratch_shapes=[pltpu.VMEM((TOP_K, tc, D), F32), pltpu.SemaphoreType.DMA(())],
        compiler_params=_cparams(("arbitrary",)),
        name="combine",
    )(dest_flat, y, x2, route_w)


def _pad_lanes(v, offset):
    return jnp.zeros((LANES,), F32).at[offset:offset + v.shape[0]].set(v.astype(F32))


def _layer(x, norm1_w, w_in, b_fox_f, fox_q_norm_w, fox_k_norm_w, gdn_conv_w, gdn_A_log,
           gdn_dt_bias, gdn_out_norm_w, w_up_fox, w_up_gdn, w_o, norm2_w, w_router, b_router,
           w_gate_up, b_gate_up, w_down, b_down):
    B, S, D = x.shape
    T = B * S
    x2d = x.reshape(T, D)

    o_f = 3 * HEADS_W
    o_gqkv = o_f + N_HEADS
    o_gb = o_gqkv + 4 * HEADS_W
    o_gate = o_gb + 2 * N_HEADS
    w_big = jnp.concatenate(
        [w_in[:, o_gate:], w_in[:, :o_f], w_in[:, o_gqkv:o_gb]], axis=1).astype(BF16)
    w_small = jnp.concatenate(
        [w_in[:, o_f:o_gqkv], w_in[:, o_gb:o_gate],
         jnp.zeros((D, LANES - 3 * N_HEADS), w_in.dtype)], axis=1).astype(BF16)

    proj, small, v_t = _in_proj(x2d, norm1_w.reshape(1, D), w_big, w_small,
                                fox_q_norm_w.reshape(1, HEAD_DIM), fox_k_norm_w.reshape(1, HEAD_DIM))

    gate_params = jnp.zeros((8, LANES), F32)
    gate_params = gate_params.at[0].set(_pad_lanes(b_fox_f, LANE_C) + _pad_lanes(gdn_dt_bias, LANE_GC))
    gate_params = gate_params.at[1].set(_pad_lanes(gdn_A_log, LANE_GC))
    gates = _gates(small, gate_params, S)
    gates_t = gates[:, :32].T
    c_bcast = jnp.broadcast_to(
        (gates_t[LANE_C:LANE_C + N_HEADS] * LOG2E)[:, :, None], (N_HEADS, T, LANES))

    y_fox = _fox(proj, v_t, c_bcast, B, S)

    u, w, qd, at, kdt, gl = _gdn_prep(proj, gdn_conv_w.astype(F32), gates, gates_t, S)
    y_gdn = _gdn_scan(u, w, qd, at, kdt, gl, proj, gdn_out_norm_w.reshape(1, HEAD_DIM), B, S)

    w_router_p = jnp.zeros((D, LANES), F32).at[:, :N_EXPERTS].set(w_router.astype(F32))
    w_router_hi = w_router_p.astype(BF16)
    w_router_lo = (w_router_p - w_router_hi.astype(F32)).astype(BF16)
    w_router_split = jnp.concatenate([w_router_hi, w_router_lo], axis=1)
    b_router_p = _pad_lanes(b_router, 0).reshape(1, LANES)
    x2, h2, logits = _merge(y_fox, y_gdn, proj, x2d, w_up_fox.astype(BF16), w_up_gdn.astype(BF16),
                            w_o.astype(BF16), norm2_w.reshape(1, D), w_router_split, b_router_p)

    route_i, route_w, counts = _route(logits)
    expert = route_i[:, :TOP_K]
    rank = route_i[:, LANE_RANK:LANE_RANK + TOP_K]
    counts = counts[0, :N_EXPERTS].astype(jnp.int32)
    padded = (counts + MOE_ROWS - 1) // MOE_ROWS * MOE_ROWS
    pad_end = jnp.cumsum(padded)
    pad_start = pad_end - padded
    dest_flat = (pad_start[expert] + rank).reshape(T * TOP_K).astype(jnp.int32)
    n_blocks = -(-T * TOP_K // MOE_ROWS) + N_EXPERTS
    n_used = (pad_end[-1:] // MOE_ROWS).astype(jnp.int32)
    block_start = jnp.arange(n_blocks, dtype=jnp.int32) * MOE_ROWS
    block_expert = jnp.minimum(
        jnp.sum((pad_end[None, :] <= block_start[:, None]).astype(jnp.int32), axis=1),
        N_EXPERTS - 1).astype(jnp.int32)

    xs = _dispatch(dest_flat, h2, jnp.zeros((n_blocks * MOE_ROWS, h2.shape[1]), h2.dtype))

    dff = w_down.shape[1]
    b_gu = b_gate_up.reshape(N_EXPERTS, 1, dff, 2).astype(F32)
    after = (pad_end[block_expert] // MOE_ROWS).astype(jnp.int32)
    next_start = jnp.where(after < n_used[0], after, -1).astype(jnp.int32)
    act = _moe_up(block_expert, n_used, next_start, xs, w_gate_up.astype(F32),
                  b_gu[..., 0], b_gu[..., 1])
    y = _moe_down(block_expert, n_used, next_start, act, w_down.astype(F32),
                  b_down.reshape(N_EXPERTS, 1, D).astype(F32))

    out = _combine(dest_flat, y, x2, route_w)
    return out.reshape(B, S, D)


def kernel(x, norm1_w, w_in, b_fox_f, fox_q_norm_w, fox_k_norm_w, gdn_conv_w, gdn_A_log, gdn_dt_bias, gdn_out_norm_w, w_up_fox, w_up_gdn, w_o, norm2_w, w_router, b_router, w_gate_up, b_gate_up, w_down, b_down):
    depth = norm1_w.shape[0]
    for l in range(depth):
        x = _layer(x, norm1_w[l], w_in[l], b_fox_f[l], fox_q_norm_w[l], fox_k_norm_w[l],
                   gdn_conv_w[l], gdn_A_log[l], gdn_dt_bias[l], gdn_out_norm_w[l], w_up_fox[l],
                   w_up_gdn[l], w_o[l], norm2_w[l], w_router[l], b_router[l], w_gate_up[l],
                   b_gate_up[l], w_down[l], b_down[l])
    return x
```

```python
import functools

import jax
import jax.numpy as jnp
from jax import lax
from jax.experimental import pallas as pl
from jax.experimental.pallas import tpu as pltpu

F32 = jnp.float32
BF16 = jnp.bfloat16

N_HEADS = 8
HEAD_DIM = 128
HEADS_W = N_HEADS * HEAD_DIM
GDN_CHUNK = 64
CONV_WIDTH = 4
N_EXPERTS = 32
TOP_K = 4
SWIGLU_LIMIT = 7.0
SWIGLU_ALPHA = 1.702
RMS_EPS = 1e-6
LANES = 128
NEG_BIG = -1e30
LOG2E = 1.4426950408889634

VMEM_LIMIT = 56 * 1024 * 1024

MOE_ROWS = 512


def _cparams(sem):
    return pltpu.CompilerParams(dimension_semantics=sem, vmem_limit_bytes=VMEM_LIMIT)


def _nt_dot(a, b):
    return lax.dot_general(a, b, (((1,), (1,)), ((), ())), preferred_element_type=F32)


def _dot(a, b):
    return jnp.dot(a, b, preferred_element_type=F32)


def _dot_exact(a, b):
    return jnp.dot(a, b, preferred_element_type=F32, precision=lax.Precision.HIGHEST)


def _silu(x):
    return x * jax.nn.sigmoid(x)


BLK_GATE_A, BLK_GATE_B = 0, 2
BLK_FQ, BLK_FK, BLK_FV = 4, 5, 6
BLK_GQ, BLK_GK, BLK_GV, BLK_GZ = 7, 8, 9, 10
CHUNK_SHIFT = GDN_CHUNK.bit_length() - 1


def _in_proj_kernel(x_ref, n1_ref, w_ref, ws_ref, qn_ref, kn_ref, proj_ref, small_ref, vt_ref,
                    h_scr):
    j = pl.program_id(1)

    @pl.when(j == 0)
    def _():
        x = x_ref[...]
        ms = jnp.mean(x * x, axis=-1, keepdims=True)
        h = (x * lax.rsqrt(ms + RMS_EPS) * n1_ref[...]).astype(BF16)
        h_scr[...] = h
        small_ref[...] = _dot(h, ws_ref[...])

    acc = _dot(h_scr[...], w_ref[...])

    def head_norm(nw_ref, scale):
        for hh in range(N_HEADS):
            a = acc[:, hh * HEAD_DIM:(hh + 1) * HEAD_DIM]
            ms = jnp.mean(a * a, axis=-1, keepdims=True)
            y = a * lax.rsqrt(ms + RMS_EPS) * nw_ref[...] * scale
            proj_ref[:, hh * HEAD_DIM:(hh + 1) * HEAD_DIM] = y.astype(BF16)

    @pl.when(j == BLK_FQ)
    def _():
        head_norm(qn_ref, HEAD_DIM ** -0.5 * LOG2E)

    @pl.when(j == BLK_FK)
    def _():
        head_norm(kn_ref, 1.0)

    @pl.when(j == BLK_FV)
    def _():
        vt_ref[...] = acc.T.astype(BF16)

    @pl.when(jnp.logical_and(j != BLK_FQ, j != BLK_FK))
    def _():
        proj_ref[...] = acc.astype(BF16)


def _in_proj(x2d, n1, w_big, w_small, qn, kn):
    T, D = x2d.shape
    n_cols = w_big.shape[1]
    tm = min(1024, T)
    tn = HEADS_W
    return pl.pallas_call(
        _in_proj_kernel,
        grid=(T // tm, n_cols // tn),
        in_specs=[
            pl.BlockSpec((tm, D), lambda i, j: (i, 0)),
            pl.BlockSpec((1, D), lambda i, j: (0, 0)),
            pl.BlockSpec((D, tn), lambda i, j: (0, j)),
            pl.BlockSpec((D, LANES), lambda i, j: (0, 0)),
            pl.BlockSpec((1, HEAD_DIM), lambda i, j: (0, 0)),
            pl.BlockSpec((1, HEAD_DIM), lambda i, j: (0, 0)),
        ],
        out_specs=[
            pl.BlockSpec((tm, tn), lambda i, j: (i, j)),
            pl.BlockSpec((tm, LANES), lambda i, j: (i, 0)),
            pl.BlockSpec((HEADS_W, tm), lambda i, j: (0, i)),
        ],
        out_shape=[
            jax.ShapeDtypeStruct((T, n_cols), BF16),
            jax.ShapeDtypeStruct((T, LANES), F32),
            jax.ShapeDtypeStruct((HEADS_W, T), BF16),
        ],
        scratch_shapes=[pltpu.VMEM((tm, D), BF16)],
        compiler_params=_cparams(("arbitrary", "arbitrary")),
        name="in_proj",
    )(x2d, n1, w_big, w_small, qn, kn)


LANE_C, LANE_BETA, LANE_GC = 0, N_HEADS, 2 * N_HEADS


def _gates_kernel(s_ref, p_ref, g_ref, carry, *, tiles_per_seq):
    i = pl.program_id(0)
    tg = s_ref.shape[0]

    @pl.when(i % tiles_per_seq == 0)
    def _():
        carry[...] = jnp.zeros_like(carry)

    z = s_ref[...] + p_ref[0:1, :]
    soft = jnp.log1p(jnp.exp(-jnp.abs(z)))
    log_sig = jnp.minimum(z, 0.0) - soft
    softplus = jnp.maximum(z, 0.0) + soft
    beta = jax.nn.sigmoid(z)
    g = -jnp.exp(p_ref[1:2, :]) * softplus

    row = lax.broadcasted_iota(jnp.int32, (tg, tg), 0)
    col = lax.broadcasted_iota(jnp.int32, (tg, tg), 1)
    tri = row >= col
    tri_chunk = jnp.logical_and(tri, (row >> CHUNK_SHIFT) == (col >> CHUNK_SHIFT))
    c = _dot_exact(tri.astype(F32), log_sig) + carry[...]
    gc = _dot_exact(tri_chunk.astype(F32), g)
    carry[...] = c[tg - 1:tg, :]

    lane = lax.broadcasted_iota(jnp.int32, (tg, LANES), 1)
    g_ref[...] = jnp.where(lane < LANE_BETA, c,
                           jnp.where(lane < LANE_GC, beta,
                                     jnp.where(lane < LANE_GC + N_HEADS, gc, 0.0)))


def _gates(small, params, seq_len):
    T = small.shape[0]
    tg = min(256, seq_len)
    return pl.pallas_call(
        functools.partial(_gates_kernel, tiles_per_seq=seq_len // tg),
        grid=(T // tg,),
        in_specs=[
            pl.BlockSpec((tg, LANES), lambda i: (i, 0)),
            pl.BlockSpec((8, LANES), lambda i: (0, 0)),
        ],
        out_specs=pl.BlockSpec((tg, LANES), lambda i: (i, 0)),
        out_shape=jax.ShapeDtypeStruct((T, LANES), F32),
        scratch_shapes=[pltpu.VMEM((1, LANES), F32)],
        compiler_params=_cparams(("arbitrary",)),
        name="gates",
    )(small, params)


FOX_TILE = 1024


def _fox_kernel(qi_ref, kj_ref, q_ref, k_ref, vt_ref, cb_ref, o_ref, m_scr, l_scr, acc_scr):
    p = pl.program_id(2)
    i = qi_ref[p]
    j = kj_ref[p]
    tq, tk = q_ref.shape[0], k_ref.shape[0]

    @pl.when(j == 0)
    def _():
        m_scr[...] = jnp.full_like(m_scr, NEG_BIG)
        l_scr[...] = jnp.zeros_like(l_scr)
        acc_scr[...] = jnp.zeros_like(acc_scr)

    def step(masked):
        st = _nt_dot(k_ref[...], q_ref[...])
        st = st - jnp.concatenate([cb_ref[...]] * (tq // LANES), axis=1)
        if masked:
            key = lax.broadcasted_iota(jnp.int32, (tk, tq), 0)
            qry = lax.broadcasted_iota(jnp.int32, (tk, tq), 1)
            st = jnp.where(qry >= key, st, -jnp.inf)
        m_old = m_scr[...]
        m_new = jnp.maximum(m_old, jnp.max(st, axis=0, keepdims=True))
        alpha = jnp.exp2(m_old - m_new)
        pt = jnp.exp2(st - m_new)
        l_scr[...] = alpha * l_scr[...] + jnp.sum(pt, axis=0, keepdims=True)
        acc_scr[...] = alpha * acc_scr[...] + _dot(vt_ref[...], pt.astype(BF16))
        m_scr[...] = m_new

    @pl.when(j < i)
    def _():
        step(False)

    @pl.when(j == i)
    def _():
        step(True)
        o_ref[...] = (acc_scr[...] / l_scr[...]).T.astype(o_ref.dtype)


def _fox(proj, v_t, c_bcast, batch, seq_len):
    T = proj.shape[0]
    tq = min(FOX_TILE, seq_len)
    nq = seq_len // tq
    pairs = [(i, j) for i in range(nq) for j in range(i + 1)]
    qi = jnp.asarray([p[0] for p in pairs], jnp.int32)
    kj = jnp.asarray([p[1] for p in pairs], jnp.int32)
    q_off, k_off = BLK_FQ * N_HEADS, BLK_FK * N_HEADS
    grid_spec = pltpu.PrefetchScalarGridSpec(
        num_scalar_prefetch=2,
        grid=(batch, N_HEADS, len(pairs)),
        in_specs=[
            pl.BlockSpec((tq, HEAD_DIM), lambda b, h, p, qi, kj: (b * nq + qi[p], q_off + h)),
            pl.BlockSpec((tq, HEAD_DIM), lambda b, h, p, qi, kj: (b * nq + kj[p], k_off + h)),
            pl.BlockSpec((HEAD_DIM, tq), lambda b, h, p, qi, kj: (h, b * nq + kj[p])),
            pl.BlockSpec((None, tq, LANES), lambda b, h, p, qi, kj: (h, b * nq + kj[p], 0)),
        ],
        out_specs=pl.BlockSpec((tq, HEAD_DIM), lambda b, h, p, qi, kj: (b * nq + qi[p], h)),
        scratch_shapes=[
            pltpu.VMEM((1, tq), F32),
            pltpu.VMEM((1, tq), F32),
            pltpu.VMEM((HEAD_DIM, tq), F32),
        ],
    )
    return pl.pallas_call(
        _fox_kernel,
        grid_spec=grid_spec,
        out_shape=jax.ShapeDtypeStruct((T, HEADS_W), BF16),
        compiler_params=_cparams(("arbitrary", "arbitrary", "arbitrary")),
        name="fox",
    )(qi, kj, proj, proj, v_t, c_bcast)


GDN_TILE = 2 * GDN_CHUNK
HALO = 16


def _gdn_prep_kernel(q_ref, k_ref, v_ref, hq_ref, hk_ref, hv_ref, cw_ref, g_ref, gt_ref,
                     u_ref, w_ref, qd_ref, at_ref, kdt_ref, gl_ref, *, tiles_per_seq):
    i = pl.program_id(0)
    R = GDN_TILE
    first = (i % tiles_per_seq) == 0
    heads = range(N_HEADS)

    def hslice(x, hh):
        return x[:, hh * HEAD_DIM:(hh + 1) * HEAD_DIM]

    dst_t = lax.broadcasted_iota(jnp.int32, (R, 2 * R), 0)
    src_t = lax.broadcasted_iota(jnp.int32, (R, 2 * R), 1)
    src_ok = jnp.logical_or(src_t >= HALO, jnp.logical_not(first))
    shifts = [jnp.logical_and(src_t == dst_t + HALO - back, src_ok).astype(BF16)
              for back in range(CONV_WIDTH - 1, 0, -1)]
    zero_rows = jnp.zeros((2 * R - HALO - R, HEADS_W), BF16)

    def conv_silu(grp, m_ref, h_ref):
        main = m_ref[...]
        stacked = jnp.concatenate([h_ref[...], main, zero_rows], axis=0)
        cw = cw_ref[:, grp * HEADS_W:(grp + 1) * HEADS_W]
        acc = main.astype(F32) * cw[CONV_WIDTH - 1:CONV_WIDTH, :]
        for tap in range(CONV_WIDTH - 1):
            acc = acc + _dot(shifts[tap], stacked) * cw[tap:tap + 1, :]
        return _silu(acc)

    q_all = conv_silu(0, q_ref, hq_ref)
    k_all = conv_silu(1, k_ref, hk_ref)
    v_all = conv_silu(2, v_ref, hv_ref)

    def l2n(x):
        return x * lax.rsqrt(jnp.sum(x * x, axis=-1, keepdims=True) + RMS_EPS)

    qn = [l2n(hslice(q_all, hh)) * (HEAD_DIM ** -0.5) for hh in heads]
    kn = [l2n(hslice(k_all, hh)) for hh in heads]
    qb = [x.astype(BF16) for x in qn]
    kb = [x.astype(BF16) for x in kn]
    vb = [hslice(v_all, hh).astype(BF16) for hh in heads]

    row = lax.broadcasted_iota(jnp.int32, (R, R), 0)
    col = lax.broadcasted_iota(jnp.int32, (R, R), 1)
    same = (row >> CHUNK_SHIFT) == (col >> CHUNK_SHIFT)
    incl = jnp.logical_and(same, row >= col)
    strict = jnp.logical_and(same, row > col)
    eye = (row == col).astype(F32)
    top = row < GDN_CHUNK

    gcol = [jnp.broadcast_to(g_ref[:, LANE_GC + hh:LANE_GC + hh + 1], (R, R)) for hh in heads]
    bcol = [jnp.broadcast_to(g_ref[:, LANE_BETA + hh:LANE_BETA + hh + 1], (R, R)) for hh in heads]
    grow = [gt_ref[LANE_GC + hh:LANE_GC + hh + 1, :] for hh in heads]
    brow = [gt_ref[LANE_BETA + hh:LANE_BETA + hh + 1, :] for hh in heads]

    decay = [jnp.where(incl, jnp.exp(jnp.minimum(gcol[hh] - grow[hh], 0.0)), 0.0) for hh in heads]
    kk = [_nt_dot(kb[hh], kb[hh]) for hh in heads]
    qk = [_nt_dot(qb[hh], kb[hh]) for hh in heads]
    lmat = [jnp.where(strict, kk[hh] * bcol[hh] * decay[hh], 0.0) for hh in heads]

    inv = [eye - lm for lm in lmat]
    power = lmat
    for _ in range(GDN_CHUNK.bit_length() - 2):
        pb = [pw.astype(BF16) for pw in power]
        power = [_dot(x, x) for x in pb]
        inv = [iv + _dot(iv.astype(BF16), pw.astype(BF16)) for iv, pw in zip(inv, power)]

    sub = lax.broadcasted_iota(jnp.int32, (8, R), 0)
    for hh in heads:
        cs = hh * HEAD_DIM
        u = _dot((inv[hh] * brow[hh]).astype(BF16), vb[hh])
        w = _dot((inv[hh] * (brow[hh] * jnp.exp(grow[hh]))).astype(BF16), kb[hh])

        g_last0 = gcol[hh][GDN_CHUNK - 1:GDN_CHUNK, :]
        g_last1 = gcol[hh][R - 1:R, :]
        g_last = jnp.where(top, g_last0, g_last1)
        q_dec = qn[hh] * jnp.exp(gcol[hh])
        k_dec = kn[hh] * jnp.exp(g_last - gcol[hh])

        attn = qk[hh] * decay[hh]
        attn_own = jnp.where(top, attn, pltpu.roll(attn, GDN_CHUNK, 1))
        attn_own = jnp.where(col < GDN_CHUNK, attn_own, 0.0)

        u_ref[:, cs:cs + HEAD_DIM] = u.astype(BF16)
        w_ref[:, cs:cs + HEAD_DIM] = w.astype(BF16)
        qd_ref[:, cs:cs + HEAD_DIM] = q_dec.astype(BF16)
        at_ref[:, cs:cs + HEAD_DIM] = attn_own.astype(BF16)
        kdt_ref[cs:cs + HEAD_DIM, :] = k_dec.T.astype(BF16)
        gl_ref[:, cs:cs + HEAD_DIM] = jnp.where(
            sub == 0, jnp.exp(g_last0), jnp.where(sub == 1, jnp.exp(g_last1), 0.0))


def _gdn_prep(proj, conv_w, gates, gates_t, seq_len):
    T = proj.shape[0]
    R = GDN_TILE
    n_tiles = T // R
    q_blk, k_blk, v_blk = BLK_GQ, BLK_GK, BLK_GV
    per_tile = R // HALO

    def main_spec(cb):
        return pl.BlockSpec((R, HEADS_W), lambda i: (i, cb))

    def halo_spec(cb):
        return pl.BlockSpec((HALO, HEADS_W), lambda i: (jnp.maximum(i * per_tile - 1, 0), cb))

    wide = pl.BlockSpec((R, HEADS_W), lambda i: (i, 0))
    return pl.pallas_call(
        functools.partial(_gdn_prep_kernel, tiles_per_seq=seq_len // R),
        grid=(n_tiles,),
        in_specs=[
            main_spec(q_blk), main_spec(k_blk), main_spec(v_blk),
            halo_spec(q_blk), halo_spec(k_blk), halo_spec(v_blk),
            pl.BlockSpec((CONV_WIDTH, 3 * HEADS_W), lambda i: (0, 0)),
            pl.BlockSpec((R, LANES), lambda i: (i, 0)),
            pl.BlockSpec((32, R), lambda i: (0, i)),
        ],
        out_specs=[
            wide, wide, wide, wide,
            pl.BlockSpec((HEADS_W, R), lambda i: (0, i)),
            pl.BlockSpec((8, HEADS_W), lambda i: (i, 0)),
        ],
        out_shape=[
            jax.ShapeDtypeStruct((T, HEADS_W), BF16),
            jax.ShapeDtypeStruct((T, HEADS_W), BF16),
            jax.ShapeDtypeStruct((T, HEADS_W), BF16),
            jax.ShapeDtypeStruct((T, HEADS_W), BF16),
            jax.ShapeDtypeStruct((HEADS_W, T), BF16),
            jax.ShapeDtypeStruct((n_tiles * 8, HEADS_W), F32),
        ],
        compiler_params=_cparams(("arbitrary",)),
        name="gdn_prep",
    )(proj, proj, proj, proj, proj, proj, conv_w, gates, gates_t)


def _gdn_scan_kernel(u_ref, w_ref, qd_ref, at_ref, kdt_ref, gl_ref, z_ref, nw_ref, y_ref, state):
    s = pl.program_id(1)
    rows = u_ref.shape[0]

    @pl.when(s == 0)
    def _():
        state[...] = jnp.zeros_like(state)

    zeros_half = jnp.zeros((GDN_CHUNK, HEAD_DIM), BF16)
    heads = range(N_HEADS)

    def blk(ref, r0, hh):
        return ref[r0:r0 + GDN_CHUNK, hh * HEAD_DIM:(hh + 1) * HEAD_DIM]

    st = [state[hh] for hh in heads]
    for c in range(rows // GDN_CHUNK):
        r0 = c * GDN_CHUNK
        tile, half = divmod(c, 2)
        t0 = tile * GDN_TILE
        sb = [x.astype(BF16) for x in st]
        wq = [_dot(jnp.concatenate([blk(w_ref, r0, hh), blk(qd_ref, r0, hh)], axis=0), sb[hh])
              for hh in heads]
        v_new = [(blk(u_ref, r0, hh).astype(F32) - wq[hh][:GDN_CHUNK]).astype(BF16) for hh in heads]
        v_lo = [jnp.concatenate([x, zeros_half], axis=0) for x in v_new]
        v_own = v_lo if half == 0 else [jnp.concatenate([zeros_half, x], axis=0) for x in v_new]
        o = [wq[hh][GDN_CHUNK:] + _dot(blk(at_ref, r0, hh), v_lo[hh]) for hh in heads]
        st = [st[hh] * gl_ref[tile * 8 + half:tile * 8 + half + 1, hh * HEAD_DIM:(hh + 1) * HEAD_DIM]
              + _dot(kdt_ref[hh * HEAD_DIM:(hh + 1) * HEAD_DIM, t0:t0 + GDN_TILE], v_own[hh])
              for hh in heads]
        for hh in heads:
            ms = jnp.mean(o[hh] * o[hh], axis=-1, keepdims=True)
            z = blk(z_ref, r0, hh).astype(F32)
            y = o[hh] * lax.rsqrt(ms + RMS_EPS) * nw_ref[...] * _silu(z)
            y_ref[r0:r0 + GDN_CHUNK, hh * HEAD_DIM:(hh + 1) * HEAD_DIM] = y.astype(y_ref.dtype)
    for hh in heads:
        state[hh] = st[hh]


def _gdn_scan(u, w, qd, at, kdt, gl, proj, out_norm_w, batch, seq_len):
    T = u.shape[0]
    rows = min(256, seq_len)
    ns = seq_len // rows
    z_blk = BLK_GZ
    wide = pl.BlockSpec((rows, HEADS_W), lambda b, s: (b * ns + s, 0))
    gl_rows = rows // GDN_TILE * 8
    return pl.pallas_call(
        _gdn_scan_kernel,
        grid=(batch, ns),
        in_specs=[
            wide, wide, wide, wide,
            pl.BlockSpec((HEADS_W, rows), lambda b, s: (0, b * ns + s)),
            pl.BlockSpec((gl_rows, HEADS_W), lambda b, s: (b * ns + s, 0)),
            pl.BlockSpec((rows, HEADS_W), lambda b, s: (b * ns + s, z_blk)),
            pl.BlockSpec((1, HEAD_DIM), lambda b, s: (0, 0)),
        ],
        out_specs=wide,
        out_shape=jax.ShapeDtypeStruct((T, HEADS_W), BF16),
        scratch_shapes=[pltpu.VMEM((N_HEADS, HEAD_DIM, HEAD_DIM), F32)],
        compiler_params=_cparams(("arbitrary", "arbitrary")),
        name="gdn_scan",
    )(u, w, qd, at, kdt, gl, proj, out_norm_w)


def _merge_kernel(yf_ref, yg_ref, ga_ref, gb_ref, x_ref, wf_ref, wg_ref, wo_ref, n2_ref,
                  wr_ref, br_ref, x2_ref, h2_ref, lg_ref):
    a = _dot(yf_ref[...], wf_ref[...])
    b = _dot(yg_ref[...], wg_ref[...])
    merged = (jax.nn.sigmoid(ga_ref[...].astype(F32)) * a
              + jax.nn.sigmoid(gb_ref[...].astype(F32)) * b)
    x2 = x_ref[...] + _dot(merged.astype(BF16), wo_ref[...])
    x2_ref[...] = x2
    ms = jnp.mean(x2 * x2, axis=-1, keepdims=True)
    h2 = x2 * lax.rsqrt(ms + RMS_EPS) * n2_ref[...]
    h_hi = h2.astype(BF16)
    h_hi32 = h_hi.astype(F32)
    h_lo = (h2 - h_hi32).astype(BF16)

    hw = _dot(h_hi, wr_ref[...])
    lg_ref[...] = (hw[:, :LANES] + hw[:, LANES:] + _dot(h_lo, wr_ref[:, :LANES])) + br_ref[...]

    half = h2.shape[1] // 2
    bits = pltpu.bitcast(h_hi32, jnp.uint32)
    h2_ref[...] = (bits[:, :half] & jnp.uint32(0xFFFF0000)) | (bits[:, half:] >> 16)


def _merge(y_fox, y_gdn, proj, x2d, wf, wg, wo, n2, wr, br):
    T, D = x2d.shape
    tm = min(256, T)
    ga_blk, gb_blk = BLK_GATE_A * HEADS_W // D, BLK_GATE_B * HEADS_W // D
    const = lambda shape: pl.BlockSpec(shape, lambda i: (0, 0), pipeline_mode=pl.Buffered(1))
    return pl.pallas_call(
        _merge_kernel,
        grid=(T // tm,),
        in_specs=[
            pl.BlockSpec((tm, HEADS_W), lambda i: (i, 0)),
            pl.BlockSpec((tm, HEADS_W), lambda i: (i, 0)),
            pl.BlockSpec((tm, D), lambda i: (i, ga_blk)),
            pl.BlockSpec((tm, D), lambda i: (i, gb_blk)),
            pl.BlockSpec((tm, D), lambda i: (i, 0)),
            const((HEADS_W, D)), const((HEADS_W, D)), const((D, D)), const((1, D)),
            const((D, 2 * LANES)), const((1, LANES)),
        ],
        out_specs=[
            pl.BlockSpec((tm, D), lambda i: (i, 0)),
            pl.BlockSpec((tm, D // 2), lambda i: (i, 0)),
            pl.BlockSpec((tm, LANES), lambda i: (i, 0)),
        ],
        out_shape=[
            jax.ShapeDtypeStruct((T, D), F32),
            jax.ShapeDtypeStruct((T, D // 2), jnp.uint32),
            jax.ShapeDtypeStruct((T, LANES), F32),
        ],
        compiler_params=_cparams(("arbitrary",)),
        name="merge",
    )(y_fox, y_gdn, proj, proj, x2d, wf, wg, wo, n2, wr, br)


LANE_RANK = TOP_K


def _route_kernel(lg_ref, ri_ref, rw_ref, cnt_ref, carry):
    i = pl.program_id(0)
    tm = lg_ref.shape[0]

    @pl.when(i == 0)
    def _():
        carry[...] = jnp.zeros_like(carry)

    lane = lax.broadcasted_iota(jnp.int32, (tm, LANES), 1)
    lane_f = lane.astype(F32)
    v = jnp.where(lane < N_EXPERTS, lg_ref[...], -jnp.inf)
    vals, idxs, hots = [], [], []
    for _ in range(TOP_K):
        m = jnp.max(v, axis=1, keepdims=True)
        idx = jnp.min(jnp.where(v == m, lane_f, float(LANES)), axis=1, keepdims=True)
        hot = lane_f == idx
        vals.append(m)
        idxs.append(idx)
        hots.append(hot)
        v = jnp.where(hot, -jnp.inf, v)

    exps = [jnp.exp(val - vals[0]) for val in vals]
    den = exps[0] + exps[1] + exps[2] + exps[3]

    multi_hot = jnp.zeros((tm, LANES), F32)
    for hot in hots:
        multi_hot = multi_hot + hot.astype(F32)
    row = lax.broadcasted_iota(jnp.int32, (tm, tm), 0)
    col = lax.broadcasted_iota(jnp.int32, (tm, tm), 1)
    before = _dot((row > col).astype(BF16), multi_hot.astype(BF16)) + carry[0:1, :]
    total = before[tm - 1:tm, :] + multi_hot[tm - 1:tm, :]
    carry[...] = jnp.broadcast_to(total, carry.shape)
    cnt_ref[...] = jnp.broadcast_to(total, cnt_ref.shape)

    out_i = jnp.zeros((tm, LANES), jnp.int32)
    out_w = jnp.zeros((tm, LANES), F32)
    for k in range(TOP_K):
        rank = jnp.sum(jnp.where(hots[k], before, 0.0), axis=1, keepdims=True).astype(jnp.int32)
        out_i = jnp.where(lane == k, idxs[k].astype(jnp.int32), out_i)
        out_i = jnp.where(lane == LANE_RANK + k, rank, out_i)
        out_w = jnp.where(lane == k, exps[k] / den, out_w)
    ri_ref[...] = out_i
    rw_ref[...] = out_w


def _route(logits):
    T = logits.shape[0]
    tm = min(256, T)
    return pl.pallas_call(
        _route_kernel,
        grid=(T // tm,),
        in_specs=[pl.BlockSpec((tm, LANES), lambda i: (i, 0))],
        out_specs=[
            pl.BlockSpec((tm, LANES), lambda i: (i, 0)),
            pl.BlockSpec((tm, LANES), lambda i: (i, 0)),
            pl.BlockSpec((8, LANES), lambda i: (0, 0)),
        ],
        out_shape=[
            jax.ShapeDtypeStruct((T, LANES), jnp.int32),
            jax.ShapeDtypeStruct((T, LANES), F32),
            jax.ShapeDtypeStruct((8, LANES), F32),
        ],
        scratch_shapes=[pltpu.VMEM((8, LANES), F32)],
        compiler_params=_cparams(("arbitrary",)),
        name="route",
    )(logits)


DISPATCH_TOKENS = 256


def _dispatch_kernel(dest_ref, h_ref, xs_in_ref, xs_ref, sem):
    del xs_in_ref

    def row_copy(r, k):
        return pltpu.make_async_copy(
            h_ref.at[pl.ds(r, 1)], xs_ref.at[pl.ds(dest_ref[r * TOP_K + k], 1)], sem)

    for r in range(DISPATCH_TOKENS):
        for k in range(TOP_K):
            row_copy(r, k).start()
    for r in range(DISPATCH_TOKENS):
        for k in range(TOP_K):
            row_copy(r, k).wait()


def _dispatch(dest_flat, h2, xs_zero):
    T, D = h2.shape
    n = DISPATCH_TOKENS * TOP_K
    return pl.pallas_call(
        _dispatch_kernel,
        grid=(T // DISPATCH_TOKENS,),
        in_specs=[
            pl.BlockSpec((n,), lambda i: (i,), memory_space=pltpu.SMEM),
            pl.BlockSpec((DISPATCH_TOKENS, D), lambda i: (i, 0)),
            pl.BlockSpec(memory_space=pl.ANY),
        ],
        out_specs=pl.BlockSpec(memory_space=pl.ANY),
        out_shape=jax.ShapeDtypeStruct(xs_zero.shape, xs_zero.dtype),
        scratch_shapes=[pltpu.SemaphoreType.DMA(())],
        input_output_aliases={2: 0},
        compiler_params=pltpu.CompilerParams(
            dimension_semantics=("arbitrary",), has_side_effects=True),
        name="dispatch",
    )(dest_flat, h2, xs_zero)


def _expert_starts(be_ref, b):
    return jnp.logical_or(b == 0, be_ref[b] != be_ref[jnp.maximum(b - 1, 0)])


UP_TILE = 1024
DOWN_TILE = 1024
DEINTERLEAVE_W = 256


def _weight_tile_stream(be_ref, nxt_ref, w_hbm, stage, sem, width, on_arrival):
    j = pl.program_id(0)
    b = pl.program_id(1)

    def tile_copy(e, jj):
        col = pl.multiple_of(jj * width, width)
        return pltpu.make_async_copy(w_hbm.at[e, :, pl.ds(col, width)], stage, sem)

    @pl.when(jnp.logical_and(j == 0, b == 0))
    def _():
        tile_copy(be_ref[0], 0).start()

    tile_copy(be_ref[b], j).wait()
    on_arrival()
    nb = nxt_ref[b]
    more_experts = nb >= 0
    e_next = be_ref[jnp.where(more_experts, nb, 0)]
    j_next = jnp.where(more_experts, j, j + 1)

    @pl.when(jnp.logical_or(more_experts, j + 1 < pl.num_programs(0)))
    def _():
        tile_copy(e_next, j_next).start()


def _moe_up_kernel(be_ref, nu_ref, nxt_ref, x_ref, w_hbm, bg_ref, bl_ref, act_ref,
                   stage, wg_scr, wl_scr, sem):
    b = pl.program_id(1)
    active = b < nu_ref[0]

    def deinterleave():
        src = lax.broadcasted_iota(jnp.int32, (DEINTERLEAVE_W, DEINTERLEAVE_W), 0)
        dst = lax.broadcasted_iota(jnp.int32, (DEINTERLEAVE_W, DEINTERLEAVE_W), 1)
        half = DEINTERLEAVE_W // 2
        want = jnp.where(dst < half, 2 * dst, 2 * (dst - half) + 1)
        sel = (src == want).astype(BF16)
        for g in range(2 * UP_TILE // DEINTERLEAVE_W):
            chunk = stage[:, g * DEINTERLEAVE_W:(g + 1) * DEINTERLEAVE_W].astype(BF16)
            moved = _dot(chunk, sel).astype(BF16)
            wg_scr[:, g * half:(g + 1) * half] = moved[:, :half]
            wl_scr[:, g * half:(g + 1) * half] = moved[:, half:]

    @pl.when(jnp.logical_and(active, _expert_starts(be_ref, b)))
    def _():
        _weight_tile_stream(be_ref, nxt_ref, w_hbm, stage, sem, 2 * UP_TILE, deinterleave)

    @pl.when(active)
    def _():
        words = x_ref[...]
        x_hi = pltpu.bitcast(words & jnp.uint32(0xFFFF0000), F32).astype(BF16)
        x_lo = pltpu.bitcast(words << 16, F32).astype(BF16)
        x = jnp.concatenate([x_hi, x_lo], axis=1)
        gate = jnp.minimum(_dot(x, wg_scr[...]) + bg_ref[...], SWIGLU_LIMIT)
        lin = jnp.clip(_dot(x, wl_scr[...]) + bl_ref[...], -SWIGLU_LIMIT, SWIGLU_LIMIT)
        act = gate * jax.nn.sigmoid(SWIGLU_ALPHA * gate) * (lin + 1.0)
        act_ref[...] = act.astype(act_ref.dtype)

    @pl.when(jnp.logical_not(active))
    def _():
        act_ref[...] = jnp.zeros_like(act_ref)


def _moe_up(block_expert, n_used, next_start, xs, w_gate_up, b_gate, b_lin):
    P = xs.shape[0]
    D = w_gate_up.shape[1]
    dff = w_gate_up.shape[2] // 2
    tn = UP_TILE
    nb = P // MOE_ROWS

    def blk(b, nu):
        return jnp.minimum(b, nu[0] - 1)

    grid_spec = pltpu.PrefetchScalarGridSpec(
        num_scalar_prefetch=3,
        grid=(dff // tn, nb),
        in_specs=[
            pl.BlockSpec((MOE_ROWS, D // 2), lambda j, b, be, nu, nx: (blk(b, nu), 0)),
            pl.BlockSpec(memory_space=pl.ANY),
            pl.BlockSpec((None, 1, tn), lambda j, b, be, nu, nx: (be[blk(b, nu)], 0, j)),
            pl.BlockSpec((None, 1, tn), lambda j, b, be, nu, nx: (be[blk(b, nu)], 0, j)),
        ],
        out_specs=pl.BlockSpec((MOE_ROWS, tn), lambda j, b, be, nu, nx: (b, j)),
        scratch_shapes=[
            pltpu.VMEM((D, 2 * tn), F32),
            pltpu.VMEM((D, tn), BF16),
            pltpu.VMEM((D, tn), BF16),
            pltpu.SemaphoreType.DMA(()),
        ],
    )
    return pl.pallas_call(
        _moe_up_kernel,
        grid_spec=grid_spec,
        out_shape=jax.ShapeDtypeStruct((P, dff), BF16),
        compiler_params=_cparams(("arbitrary", "arbitrary")),
        name="moe_up",
    )(block_expert, n_used, next_start, xs, w_gate_up, b_gate, b_lin)


def _moe_down_kernel(be_ref, nu_ref, nxt_ref, a_ref, w_hbm, bd_ref, y_ref, stage, wd_scr, sem):
    b = pl.program_id(1)
    active = b < nu_ref[0]

    def to_bf16():
        wd_scr[...] = stage[...].astype(BF16)

    @pl.when(jnp.logical_and(active, _expert_starts(be_ref, b)))
    def _():
        _weight_tile_stream(be_ref, nxt_ref, w_hbm, stage, sem, DOWN_TILE, to_bf16)

    @pl.when(active)
    def _():
        y_ref[...] = _dot(a_ref[...], wd_scr[...]) + bd_ref[...]

    @pl.when(jnp.logical_not(active))
    def _():
        y_ref[...] = jnp.zeros_like(y_ref)


def _moe_down(block_expert, n_used, next_start, act, w_down, b_down):
    P, dff = act.shape
    D = w_down.shape[2]
    tn = DOWN_TILE
    nb = P // MOE_ROWS

    def blk(b, nu):
        return jnp.minimum(b, nu[0] - 1)

    grid_spec = pltpu.PrefetchScalarGridSpec(
        num_scalar_prefetch=3,
        grid=(D // tn, nb),
        in_specs=[
            pl.BlockSpec((MOE_ROWS, dff), lambda j, b, be, nu, nx: (blk(b, nu), 0)),
            pl.BlockSpec(memory_space=pl.ANY),
            pl.BlockSpec((None, 1, tn), lambda j, b, be, nu, nx: (be[blk(b, nu)], 0, j)),
        ],
        out_specs=pl.BlockSpec((MOE_ROWS, tn), lambda j, b, be, nu, nx: (b, j)),
        scratch_shapes=[
            pltpu.VMEM((dff, tn), F32),
            pltpu.VMEM((dff, tn), BF16),
            pltpu.SemaphoreType.DMA(()),
        ],
    )
    return pl.pallas_call(
        _moe_down_kernel,
        grid_spec=grid_spec,
        out_shape=jax.ShapeDtypeStruct((P, D), F32),
        compiler_params=_cparams(("arbitrary", "arbitrary")),
        name="moe_down",
    )(block_expert, n_used, next_start, act, w_down, b_down)


COMBINE_TOKENS = 256


def _combine_kernel(dest_ref, y_ref, x2_ref, rw_ref, o_ref, buf, sem):
    def row_copy(r, k):
        return pltpu.make_async_copy(
            y_ref.at[pl.ds(dest_ref[r * TOP_K + k], 1)], buf.at[k, pl.ds(r, 1)], sem)

    for r in range(COMBINE_TOKENS):
        for k in range(TOP_K):
            row_copy(r, k).start()
    for r in range(COMBINE_TOKENS):
        for k in range(TOP_K):
            row_copy(r, k).wait()
    acc = x2_ref[...]
    for k in range(TOP_K):
        acc = acc + rw_ref[:, k:k + 1] * buf[k]
    o_ref[...] = acc


def _combine(dest_flat, y, x2, route_w):
    T, D = x2.shape
    tc = COMBINE_TOKENS
    return pl.pallas_call(
        _combine_kernel,
        grid=(T // tc,),
        in_specs=[
            pl.BlockSpec((tc * TOP_K,), lambda i: (i,), memory_space=pltpu.SMEM),
            pl.BlockSpec(memory_space=pl.ANY),
            pl.BlockSpec((tc, D), lambda i: (i, 0)),
            pl.BlockSpec((tc, LANES), lambda i: (i, 0)),
        ],
        out_specs=pl.BlockSpec((tc, D), lambda i: (i, 0)),
        out_shape=jax.ShapeDtypeStruct((T, D), F32),
        scratch_shapes=[pltpu.VMEM((TOP_K, tc, D), F32), pltpu.SemaphoreType.DMA(())],
        compiler_params=_cparams(("arbitrary",)),
        name="combine",
    )(dest_flat, y, x2, route_w)


def _pad_lanes(v, offset):
    return jnp.zeros((LANES,), F32).at[offset:offset + v.shape[0]].set(v.astype(F32))


def _layer(x, norm1_w, w_in, b_fox_f, fox_q_norm_w, fox_k_norm_w, gdn_conv_w, gdn_A_log,
           gdn_dt_bias, gdn_out_norm_w, w_up_fox, w_up_gdn, w_o, norm2_w, w_router, b_router,
           w_gate_up, b_gate_up, w_down, b_down):
    B, S, D = x.shape
    T = B * S
    x2d = x.reshape(T, D)

    o_f = 3 * HEADS_W
    o_gqkv = o_f + N_HEADS
    o_gb = o_gqkv + 4 * HEADS_W
    o_gate = o_gb + 2 * N_HEADS
    w_big = jnp.concatenate(
        [w_in[:, o_gate:], w_in[:, :o_f], w_in[:, o_gqkv:o_gb]], axis=1).astype(BF16)
    w_small = jnp.concatenate(
        [w_in[:, o_f:o_gqkv], w_in[:, o_gb:o_gate],
         jnp.zeros((D, LANES - 3 * N_HEADS), w_in.dtype)], axis=1).astype(BF16)

    proj, small, v_t = _in_proj(x2d, norm1_w.reshape(1, D), w_big, w_small,
                                fox_q_norm_w.reshape(1, HEAD_DIM), fox_k_norm_w.reshape(1, HEAD_DIM))

    gate_params = jnp.zeros((8, LANES), F32)
    gate_params = gate_params.at[0].set(_pad_lanes(b_fox_f, LANE_C) + _pad_lanes(gdn_dt_bias, LANE_GC))
    gate_params = gate_params.at[1].set(_pad_lanes(gdn_A_log, LANE_GC))
    gates = _gates(small, gate_params, S)
    gates_t = gates[:, :32].T
    c_bcast = jnp.broadcast_to(
        (gates_t[LANE_C:LANE_C + N_HEADS] * LOG2E)[:, :, None], (N_HEADS, T, LANES))

    y_fox = _fox(proj, v_t, c_bcast, B, S)

    u, w, qd, at, kdt, gl = _gdn_prep(proj, gdn_conv_w.astype(F32), gates, gates_t, S)
    y_gdn = _gdn_scan(u, w, qd, at, kdt, gl, proj, gdn_out_norm_w.reshape(1, HEAD_DIM), B, S)

    w_router_p = jnp.zeros((D, LANES), F32).at[:, :N_EXPERTS].set(w_router.astype(F32))
    w_router_hi = w_router_p.astype(BF16)
    w_router_lo = (w_router_p - w_router_hi.astype(F32)).astype(BF16)
    w_router_split = jnp.concatenate([w_router_hi, w_router_lo], axis=1)
    b_router_p = _pad_lanes(b_router, 0).reshape(1, LANES)
    x2, h2, logits = _merge(y_fox, y_gdn, proj, x2d, w_up_fox.astype(BF16), w_up_gdn.astype(BF16),
                            w_o.astype(BF16), norm2_w.reshape(1, D), w_router_split, b_router_p)

    route_i, route_w, counts = _route(logits)
    expert = route_i[:, :TOP_K]
    rank = route_i[:, LANE_RANK:LANE_RANK + TOP_K]
    counts = counts[0, :N_EXPERTS].astype(jnp.int32)
    padded = (counts + MOE_ROWS - 1) // MOE_ROWS * MOE_ROWS
    pad_end = jnp.cumsum(padded)
    pad_start = pad_end - padded
    dest_flat = (pad_start[expert] + rank).reshape(T * TOP_K).astype(jnp.int32)
    n_blocks = -(-T * TOP_K // MOE_ROWS) + N_EXPERTS
    n_used = (pad_end[-1:] // MOE_ROWS).astype(jnp.int32)
    block_start = jnp.arange(n_blocks, dtype=jnp.int32) * MOE_ROWS
    block_expert = jnp.minimum(
        jnp.sum((pad_end[None, :] <= block_start[:, None]).astype(jnp.int32), axis=1),
        N_EXPERTS - 1).astype(jnp.int32)

    xs = _dispatch(dest_flat, h2, jnp.zeros((n_blocks * MOE_ROWS, h2.shape[1]), h2.dtype))

    dff = w_down.shape[1]
    b_gu = b_gate_up.reshape(N_EXPERTS, 1, dff, 2).astype(F32)
    after = (pad_end[block_expert] // MOE_ROWS).astype(jnp.int32)
    next_start = jnp.where(after < n_used[0], after, -1).astype(jnp.int32)
    act = _moe_up(block_expert, n_used, next_start, xs, w_gate_up.astype(F32),
                  b_gu[..., 0], b_gu[..., 1])
    y = _moe_down(block_expert, n_used, next_start, act, w_down.astype(F32),
                  b_down.reshape(N_EXPERTS, 1, D).astype(F32))

    out = _combine(dest_flat, y, x2, route_w)
    return out.reshape(B, S, D)


def kernel(x, norm1_w, w_in, b_fox_f, fox_q_norm_w, fox_k_norm_w, gdn_conv_w, gdn_A_log, gdn_dt_bias, gdn_out_norm_w, w_up_fox, w_up_gdn, w_o, norm2_w, w_router, b_router, w_gate_up, b_gate_up, w_down, b_down):
    depth = norm1_w.shape[0]
    for l in range(depth):
        x = _layer(x, norm1_w[l], w_in[l], b_fox_f[l], fox_q_norm_w[l], fox_k_norm_w[l],
                   gdn_conv_w[l], gdn_A_log[l], gdn_dt_bias[l], gdn_out_norm_w[l], w_up_fox[l],
                   w_up_gdn[l], w_o[l], norm2_w[l], w_router[l], b_router[l], w_gate_up[l],
                   b_gate_up[l], w_down[l], b_down[l])
    return x
```

```python
import functools

import jax
import jax.numpy as jnp
from jax import lax
from jax.experimental import pallas as pl
from jax.experimental.pallas import tpu as pltpu

F32 = jnp.float32
BF16 = jnp.bfloat16

N_HEADS = 8
HEAD_DIM = 128
HEADS_W = N_HEADS * HEAD_DIM
GDN_CHUNK = 64
CONV_WIDTH = 4
N_EXPERTS = 32
TOP_K = 4
SWIGLU_LIMIT = 7.0
SWIGLU_ALPHA = 1.702
RMS_EPS = 1e-6
LANES = 128
NEG_BIG = -1e30
LOG2E = 1.4426950408889634

VMEM_LIMIT = 56 * 1024 * 1024

MOE_ROWS = 512


def _cparams(sem):
    return pltpu.CompilerParams(dimension_semantics=sem, vmem_limit_bytes=VMEM_LIMIT)


def _nt_dot(a, b):
    return lax.dot_general(a, b, (((1,), (1,)), ((), ())), preferred_element_type=F32)


def _dot(a, b):
    return jnp.dot(a, b, preferred_element_type=F32)


def _dot_exact(a, b):
    return jnp.dot(a, b, preferred_element_type=F32, precision=lax.Precision.HIGHEST)


def _silu(x):
    return x * jax.nn.sigmoid(x)


BLK_GATE_A, BLK_GATE_B = 0, 2
BLK_FQ, BLK_FK, BLK_FV = 4, 5, 6
BLK_GQ, BLK_GK, BLK_GV, BLK_GZ = 7, 8, 9, 10
CHUNK_SHIFT = GDN_CHUNK.bit_length() - 1


def _in_proj_kernel(x_ref, n1_ref, w_ref, ws_ref, qn_ref, kn_ref, proj_ref, small_ref, vt_ref,
                    h_scr):
    j = pl.program_id(1)

    @pl.when(j == 0)
    def _():
        x = x_ref[...]
        ms = jnp.mean(x * x, axis=-1, keepdims=True)
        h = (x * lax.rsqrt(ms + RMS_EPS) * n1_ref[...]).astype(BF16)
        h_scr[...] = h
        small_ref[...] = _dot(h, ws_ref[...])

    acc = _dot(h_scr[...], w_ref[...])

    def head_norm(nw_ref, scale):
        for hh in range(N_HEADS):
            a = acc[:, hh * HEAD_DIM:(hh + 1) * HEAD_DIM]
            ms = jnp.mean(a * a, axis=-1, keepdims=True)
            y = a * lax.rsqrt(ms + RMS_EPS) * nw_ref[...] * scale
            proj_ref[:, hh * HEAD_DIM:(hh + 1) * HEAD_DIM] = y.astype(BF16)

    @pl.when(j == BLK_FQ)
    def _():
        head_norm(qn_ref, HEAD_DIM ** -0.5 * LOG2E)

    @pl.when(j == BLK_FK)
    def _():
        head_norm(kn_ref, 1.0)

    @pl.when(j == BLK_FV)
    def _():
        vt_ref[...] = acc.T.astype(BF16)

    @pl.when(jnp.logical_and(j != BLK_FQ, j != BLK_FK))
    def _():
        proj_ref[...] = acc.astype(BF16)


def _in_proj(x2d, n1, w_big, w_small, qn, kn):
    T, D = x2d.shape
    n_cols = w_big.shape[1]
    tm = min(1024, T)
    tn = HEADS_W
    return pl.pallas_call(
        _in_proj_kernel,
        grid=(T // tm, n_cols // tn),
        in_specs=[
            pl.BlockSpec((tm, D), lambda i, j: (i, 0)),
            pl.BlockSpec((1, D), lambda i, j: (0, 0)),
            pl.BlockSpec((D, tn), lambda i, j: (0, j)),
            pl.BlockSpec((D, LANES), lambda i, j: (0, 0)),
            pl.BlockSpec((1, HEAD_DIM), lambda i, j: (0, 0)),
            pl.BlockSpec((1, HEAD_DIM), lambda i, j: (0, 0)),
        ],
        out_specs=[
            pl.BlockSpec((tm, tn), lambda i, j: (i, j)),
            pl.BlockSpec((tm, LANES), lambda i, j: (i, 0)),
            pl.BlockSpec((HEADS_W, tm), lambda i, j: (0, i)),
        ],
        out_shape=[
            jax.ShapeDtypeStruct((T, n_cols), BF16),
            jax.ShapeDtypeStruct((T, LANES), F32),
            jax.ShapeDtypeStruct((HEADS_W, T), BF16),
        ],
        scratch_shapes=[pltpu.VMEM((tm, D), BF16)],
        compiler_params=_cparams(("arbitrary", "arbitrary")),
        name="in_proj",
    )(x2d, n1, w_big, w_small, qn, kn)


LANE_C, LANE_BETA, LANE_GC = 0, N_HEADS, 2 * N_HEADS


def _gates_kernel(s_ref, p_ref, g_ref, carry, *, tiles_per_seq):
    i = pl.program_id(0)
    tg = s_ref.shape[0]

    @pl.when(i % tiles_per_seq == 0)
    def _():
        carry[...] = jnp.zeros_like(carry)

    z = s_ref[...] + p_ref[0:1, :]
    soft = jnp.log1p(jnp.exp(-jnp.abs(z)))
    log_sig = jnp.minimum(z, 0.0) - soft
    softplus = jnp.maximum(z, 0.0) + soft
    beta = jax.nn.sigmoid(z)
    g = -jnp.exp(p_ref[1:2, :]) * softplus

    row = lax.broadcasted_iota(jnp.int32, (tg, tg), 0)
    col = lax.broadcasted_iota(jnp.int32, (tg, tg), 1)
    tri = row >= col
    tri_chunk = jnp.logical_and(tri, (row >> CHUNK_SHIFT) == (col >> CHUNK_SHIFT))
    c = _dot_exact(tri.astype(F32), log_sig) + carry[...]
    gc = _dot_exact(tri_chunk.astype(F32), g)
    carry[...] = c[tg - 1:tg, :]

    lane = lax.broadcasted_iota(jnp.int32, (tg, LANES), 1)
    g_ref[...] = jnp.where(lane < LANE_BETA, c,
                           jnp.where(lane < LANE_GC, beta,
                                     jnp.where(lane < LANE_GC + N_HEADS, gc, 0.0)))


def _gates(small, params, seq_len):
    T = small.shape[0]
    tg = min(256, seq_len)
    return pl.pallas_call(
        functools.partial(_gates_kernel, tiles_per_seq=seq_len // tg),
        grid=(T // tg,),
        in_specs=[
            pl.BlockSpec((tg, LANES), lambda i: (i, 0)),
            pl.BlockSpec((8, LANES), lambda i: (0, 0)),
        ],
        out_specs=pl.BlockSpec((tg, LANES), lambda i: (i, 0)),
        out_shape=jax.ShapeDtypeStruct((T, LANES), F32),
        scratch_shapes=[pltpu.VMEM((1, LANES), F32)],
        compiler_params=_cparams(("arbitrary",)),
        name="gates",
    )(small, params)


FOX_TILE = 1024


def _fox_kernel(qi_ref, kj_ref, q_ref, k_ref, vt_ref, cb_ref, o_ref, m_scr, l_scr, acc_scr):
    p = pl.program_id(2)
    i = qi_ref[p]
    j = kj_ref[p]
    tq, tk = q_ref.shape[0], k_ref.shape[0]

    @pl.when(j == 0)
    def _():
        m_scr[...] = jnp.full_like(m_scr, NEG_BIG)
        l_scr[...] = jnp.zeros_like(l_scr)
        acc_scr[...] = jnp.zeros_like(acc_scr)

    def step(masked):
        st = _nt_dot(k_ref[...], q_ref[...])
        st = st - jnp.concatenate([cb_ref[...]] * (tq // LANES), axis=1)
        if masked:
            key = lax.broadcasted_iota(jnp.int32, (tk, tq), 0)
            qry = lax.broadcasted_iota(jnp.int32, (tk, tq), 1)
            st = jnp.where(qry >= key, st, -jnp.inf)
        m_old = m_scr[...]
        m_new = jnp.maximum(m_old, jnp.max(st, axis=0, keepdims=True))
        alpha = jnp.exp2(m_old - m_new)
        pt = jnp.exp2(st - m_new)
        l_scr[...] = alpha * l_scr[...] + jnp.sum(pt, axis=0, keepdims=True)
        acc_scr[...] = alpha * acc_scr[...] + _dot(vt_ref[...], pt.astype(BF16))
        m_scr[...] = m_new

    @pl.when(j < i)
    def _():
        step(False)

    @pl.when(j == i)
    def _():
        step(True)
        o_ref[...] = (acc_scr[...] / l_scr[...]).T.astype(o_ref.dtype)


def _fox(proj, v_t, c_bcast, batch, seq_len):
    T = proj.shape[0]
    tq = min(FOX_TILE, seq_len)
    nq = seq_len // tq
    pairs = [(i, j) for i in range(nq) for j in range(i + 1)]
    qi = jnp.asarray([p[0] for p in pairs], jnp.int32)
    kj = jnp.asarray([p[1] for p in pairs], jnp.int32)
    q_off, k_off = BLK_FQ * N_HEADS, BLK_FK * N_HEADS
    grid_spec = pltpu.PrefetchScalarGridSpec(
        num_scalar_prefetch=2,
        grid=(batch, N_HEADS, len(pairs)),
        in_specs=[
            pl.BlockSpec((tq, HEAD_DIM), lambda b, h, p, qi, kj: (b * nq + qi[p], q_off + h)),
            pl.BlockSpec((tq, HEAD_DIM), lambda b, h, p, qi, kj: (b * nq + kj[p], k_off + h)),
            pl.BlockSpec((HEAD_DIM, tq), lambda b, h, p, qi, kj: (h, b * nq + kj[p])),
            pl.BlockSpec((None, tq, LANES), lambda b, h, p, qi, kj: (h, b * nq + kj[p], 0)),
        ],
        out_specs=pl.BlockSpec((tq, HEAD_DIM), lambda b, h, p, qi, kj: (b * nq + qi[p], h)),
        scratch_shapes=[
            pltpu.VMEM((1, tq), F32),
            pltpu.VMEM((1, tq), F32),
            pltpu.VMEM((HEAD_DIM, tq), F32),
        ],
    )
    return pl.pallas_call(
        _fox_kernel,
        grid_spec=grid_spec,
        out_shape=jax.ShapeDtypeStruct((T, HEADS_W), BF16),
        compiler_params=_cparams(("arbitrary", "arbitrary", "arbitrary")),
        name="fox",
    )(qi, kj, proj, proj, v_t, c_bcast)


GDN_TILE = 2 * GDN_CHUNK
HALO = 16


def _gdn_prep_kernel(q_ref, k_ref, v_ref, hq_ref, hk_ref, hv_ref, cw_ref, g_ref, gt_ref,
                     u_ref, w_ref, qd_ref, at_ref, kdt_ref, gl_ref, *, tiles_per_seq):
    i = pl.program_id(0)
    R = GDN_TILE
    first = (i % tiles_per_seq) == 0
    heads = range(N_HEADS)

    def hslice(x, hh):
        return x[:, hh * HEAD_DIM:(hh + 1) * HEAD_DIM]

    dst_t = lax.broadcasted_iota(jnp.int32, (R, 2 * R), 0)
    src_t = lax.broadcasted_iota(jnp.int32, (R, 2 * R), 1)
    src_ok = jnp.logical_or(src_t >= HALO, jnp.logical_not(first))
    shifts = [jnp.logical_and(src_t == dst_t + HALO - back, src_ok).astype(BF16)
              for back in range(CONV_WIDTH - 1, 0, -1)]
    zero_rows = jnp.zeros((2 * R - HALO - R, HEADS_W), BF16)

    def conv_silu(grp, m_ref, h_ref):
        main = m_ref[...]
        stacked = jnp.concatenate([h_ref[...], main, zero_rows], axis=0)
        cw = cw_ref[:, grp * HEADS_W:(grp + 1) * HEADS_W]
        acc = main.astype(F32) * cw[CONV_WIDTH - 1:CONV_WIDTH, :]
        for tap in range(CONV_WIDTH - 1):
            acc = acc + _dot(shifts[tap], stacked) * cw[tap:tap + 1, :]
        return _silu(acc)

    q_all = conv_silu(0, q_ref, hq_ref)
    k_all = conv_silu(1, k_ref, hk_ref)
    v_all = conv_silu(2, v_ref, hv_ref)

    def l2n(x):
        return x * lax.rsqrt(jnp.sum(x * x, axis=-1, keepdims=True) + RMS_EPS)

    qn = [l2n(hslice(q_all, hh)) * (HEAD_DIM ** -0.5) for hh in heads]
    kn = [l2n(hslice(k_all, hh)) for hh in heads]
    qb = [x.astype(BF16) for x in qn]
    kb = [x.astype(BF16) for x in kn]
    vb = [hslice(v_all, hh).astype(BF16) for hh in heads]

    row = lax.broadcasted_iota(jnp.int32, (R, R), 0)
    col = lax.broadcasted_iota(jnp.int32, (R, R), 1)
    same = (row >> CHUNK_SHIFT) == (col >> CHUNK_SHIFT)
    incl = jnp.logical_and(same, row >= col)
    strict = jnp.logical_and(same, row > col)
    eye = (row == col).astype(F32)
    top = row < GDN_CHUNK

    gcol = [jnp.broadcast_to(g_ref[:, LANE_GC + hh:LANE_GC + hh + 1], (R, R)) for hh in heads]
    bcol = [jnp.broadcast_to(g_ref[:, LANE_BETA + hh:LANE_BETA + hh + 1], (R, R)) for hh in heads]
    grow = [gt_ref[LANE_GC + hh:LANE_GC + hh + 1, :] for hh in heads]
    brow = [gt_ref[LANE_BETA + hh:LANE_BETA + hh + 1, :] for hh in heads]

    decay = [jnp.where(incl, jnp.exp(jnp.minimum(gcol[hh] - grow[hh], 0.0)), 0.0) for hh in heads]
    kk = [_nt_dot(kb[hh], kb[hh]) for hh in heads]
    qk = [_nt_dot(qb[hh], kb[hh]) for hh in heads]
    lmat = [jnp.where(strict, kk[hh] * bcol[hh] * decay[hh], 0.0) for hh in heads]

    inv = [eye - lm for lm in lmat]
    power = lmat
    for _ in range(GDN_CHUNK.bit_length() - 2):
        pb = [pw.astype(BF16) for pw in power]
        power = [_dot(x, x) for x in pb]
        inv = [iv + _dot(iv.astype(BF16), pw.astype(BF16)) for iv, pw in zip(inv, power)]

    sub = lax.broadcasted_iota(jnp.int32, (8, R), 0)
    for hh in heads:
        cs = hh * HEAD_DIM
        u = _dot((inv[hh] * brow[hh]).astype(BF16), vb[hh])
        w = _dot((inv[hh] * (brow[hh] * jnp.exp(grow[hh]))).astype(BF16), kb[hh])

        g_last0 = gcol[hh][GDN_CHUNK - 1:GDN_CHUNK, :]
        g_last1 = gcol[hh][R - 1:R, :]
        g_last = jnp.where(top, g_last0, g_last1)
        q_dec = qn[hh] * jnp.exp(gcol[hh])
        k_dec = kn[hh] * jnp.exp(g_last - gcol[hh])

        attn = qk[hh] * decay[hh]
        attn_own = jnp.where(top, attn, pltpu.roll(attn, GDN_CHUNK, 1))
        attn_own = jnp.where(col < GDN_CHUNK, attn_own, 0.0)

        u_ref[:, cs:cs + HEAD_DIM] = u.astype(BF16)
        w_ref[:, cs:cs + HEAD_DIM] = w.astype(BF16)
        qd_ref[:, cs:cs + HEAD_DIM] = q_dec.astype(BF16)
        at_ref[:, cs:cs + HEAD_DIM] = attn_own.astype(BF16)
        kdt_ref[cs:cs + HEAD_DIM, :] = k_dec.T.astype(BF16)
        gl_ref[:, cs:cs + HEAD_DIM] = jnp.where(
            sub == 0, jnp.exp(g_last0), jnp.where(sub == 1, jnp.exp(g_last1), 0.0))


def _gdn_prep(proj, conv_w, gates, gates_t, seq_len):
    T = proj.shape[0]
    R = GDN_TILE
    n_tiles = T // R
    q_blk, k_blk, v_blk = BLK_GQ, BLK_GK, BLK_GV
    per_tile = R // HALO

    def main_spec(cb):
        return pl.BlockSpec((R, HEADS_W), lambda i: (i, cb))

    def halo_spec(cb):
        return pl.BlockSpec((HALO, HEADS_W), lambda i: (jnp.maximum(i * per_tile - 1, 0), cb))

    wide = pl.BlockSpec((R, HEADS_W), lambda i: (i, 0))
    return pl.pallas_call(
        functools.partial(_gdn_prep_kernel, tiles_per_seq=seq_len // R),
        grid=(n_tiles,),
        in_specs=[
            main_spec(q_blk), main_spec(k_blk), main_spec(v_blk),
            halo_spec(q_blk), halo_spec(k_blk), halo_spec(v_blk),
            pl.BlockSpec((CONV_WIDTH, 3 * HEADS_W), lambda i: (0, 0)),
            pl.BlockSpec((R, LANES), lambda i: (i, 0)),
            pl.BlockSpec((32, R), lambda i: (0, i)),
        ],
        out_specs=[
            wide, wide, wide, wide,
            pl.BlockSpec((HEADS_W, R), lambda i: (0, i)),
            pl.BlockSpec((8, HEADS_W), lambda i: (i, 0)),
        ],
        out_shape=[
            jax.ShapeDtypeStruct((T, HEADS_W), BF16),
            jax.ShapeDtypeStruct((T, HEADS_W), BF16),
            jax.ShapeDtypeStruct((T, HEADS_W), BF16),
            jax.ShapeDtypeStruct((T, HEADS_W), BF16),
            jax.ShapeDtypeStruct((HEADS_W, T), BF16),
            jax.ShapeDtypeStruct((n_tiles * 8, HEADS_W), F32),
        ],
        compiler_params=_cparams(("arbitrary",)),
        name="gdn_prep",
    )(proj, proj, proj, proj, proj, proj, conv_w, gates, gates_t)


def _gdn_scan_kernel(u_ref, w_ref, qd_ref, at_ref, kdt_ref, gl_ref, z_ref, nw_ref, y_ref, state):
    s = pl.program_id(1)
    rows = u_ref.shape[0]

    @pl.when(s == 0)
    def _():
        state[...] = jnp.zeros_like(state)

    zeros_half = jnp.zeros((GDN_CHUNK, HEAD_DIM), BF16)
    heads = range(N_HEADS)

    def blk(ref, r0, hh):
        return ref[r0:r0 + GDN_CHUNK, hh * HEAD_DIM:(hh + 1) * HEAD_DIM]

    st = [state[hh] for hh in heads]
    for c in range(rows // GDN_CHUNK):
        r0 = c * GDN_CHUNK
        tile, half = divmod(c, 2)
        t0 = tile * GDN_TILE
        sb = [x.astype(BF16) for x in st]
        wq = [_dot(jnp.concatenate([blk(w_ref, r0, hh), blk(qd_ref, r0, hh)], axis=0), sb[hh])
              for hh in heads]
        v_new = [(blk(u_ref, r0, hh).astype(F32) - wq[hh][:GDN_CHUNK]).astype(BF16) for hh in heads]
        v_lo = [jnp.concatenate([x, zeros_half], axis=0) for x in v_new]
        v_own = v_lo if half == 0 else [jnp.concatenate([zeros_half, x], axis=0) for x in v_new]
        o = [wq[hh][GDN_CHUNK:] + _dot(blk(at_ref, r0, hh), v_lo[hh]) for hh in heads]
        st = [st[hh] * gl_ref[tile * 8 + half:tile * 8 + half + 1, hh * HEAD_DIM:(hh + 1) * HEAD_DIM]
              + _dot(kdt_ref[hh * HEAD_DIM:(hh + 1) * HEAD_DIM, t0:t0 + GDN_TILE], v_own[hh])
              for hh in heads]
        for hh in heads:
            ms = jnp.mean(o[hh] * o[hh], axis=-1, keepdims=True)
            z = blk(z_ref, r0, hh).astype(F32)
            y = o[hh] * lax.rsqrt(ms + RMS_EPS) * nw_ref[...] * _silu(z)
            y_ref[r0:r0 + GDN_CHUNK, hh * HEAD_DIM:(hh + 1) * HEAD_DIM] = y.astype(y_ref.dtype)
    for hh in heads:
        state[hh] = st[hh]


def _gdn_scan(u, w, qd, at, kdt, gl, proj, out_norm_w, batch, seq_len):
    T = u.shape[0]
    rows = min(256, seq_len)
    ns = seq_len // rows
    z_blk = BLK_GZ
    wide = pl.BlockSpec((rows, HEADS_W), lambda b, s: (b * ns + s, 0))
    gl_rows = rows // GDN_TILE * 8
    return pl.pallas_call(
        _gdn_scan_kernel,
        grid=(batch, ns),
        in_specs=[
            wide, wide, wide, wide,
            pl.BlockSpec((HEADS_W, rows), lambda b, s: (0, b * ns + s)),
            pl.BlockSpec((gl_rows, HEADS_W), lambda b, s: (b * ns + s, 0)),
            pl.BlockSpec((rows, HEADS_W), lambda b, s: (b * ns + s, z_blk)),
            pl.BlockSpec((1, HEAD_DIM), lambda b, s: (0, 0)),
        ],
        out_specs=wide,
        out_shape=jax.ShapeDtypeStruct((T, HEADS_W), BF16),
        scratch_shapes=[pltpu.VMEM((N_HEADS, HEAD_DIM, HEAD_DIM), F32)],
        compiler_params=_cparams(("arbitrary", "arbitrary")),
        name="gdn_scan",
    )(u, w, qd, at, kdt, gl, proj, out_norm_w)


def _merge_kernel(yf_ref, yg_ref, ga_ref, gb_ref, x_ref, wf_ref, wg_ref, wo_ref, n2_ref,
                  wr_ref, br_ref, x2_ref, h2_ref, lg_ref):
    a = _dot(yf_ref[...], wf_ref[...])
    b = _dot(yg_ref[...], wg_ref[...])
    merged = (jax.nn.sigmoid(ga_ref[...].astype(F32)) * a
              + jax.nn.sigmoid(gb_ref[...].astype(F32)) * b)
    x2 = x_ref[...] + _dot(merged.astype(BF16), wo_ref[...])
    x2_ref[...] = x2
    ms = jnp.mean(x2 * x2, axis=-1, keepdims=True)
    h2 = x2 * lax.rsqrt(ms + RMS_EPS) * n2_ref[...]
    h_hi = h2.astype(BF16)
    h_hi32 = h_hi.astype(F32)
    h_lo = (h2 - h_hi32).astype(BF16)

    hw = _dot(h_hi, wr_ref[...])
    lg_ref[...] = (hw[:, :LANES] + hw[:, LANES:] + _dot(h_lo, wr_ref[:, :LANES])) + br_ref[...]

    half = h2.shape[1] // 2
    bits = pltpu.bitcast(h_hi32, jnp.uint32)
    h2_ref[...] = (bits[:, :half] & jnp.uint32(0xFFFF0000)) | (bits[:, half:] >> 16)


def _merge(y_fox, y_gdn, proj, x2d, wf, wg, wo, n2, wr, br):
    T, D = x2d.shape
    tm = min(256, T)
    ga_blk, gb_blk = BLK_GATE_A * HEADS_W // D, BLK_GATE_B * HEADS_W // D
    const = lambda shape: pl.BlockSpec(shape, lambda i: (0, 0), pipeline_mode=pl.Buffered(1))
    return pl.pallas_call(
        _merge_kernel,
        grid=(T // tm,),
        in_specs=[
            pl.BlockSpec((tm, HEADS_W), lambda i: (i, 0)),
            pl.BlockSpec((tm, HEADS_W), lambda i: (i, 0)),
            pl.BlockSpec((tm, D), lambda i: (i, ga_blk)),
            pl.BlockSpec((tm, D), lambda i: (i, gb_blk)),
            pl.BlockSpec((tm, D), lambda i: (i, 0)),
            const((HEADS_W, D)), const((HEADS_W, D)), const((D, D)), const((1, D)),
            const((D, 2 * LANES)), const((1, LANES)),
        ],
        out_specs=[
            pl.BlockSpec((tm, D), lambda i: (i, 0)),
            pl.BlockSpec((tm, D // 2), lambda i: (i, 0)),
            pl.BlockSpec((tm, LANES), lambda i: (i, 0)),
        ],
        out_shape=[
            jax.ShapeDtypeStruct((T, D), F32),
            jax.ShapeDtypeStruct((T, D // 2), jnp.uint32),
            jax.ShapeDtypeStruct((T, LANES), F32),
        ],
        compiler_params=_cparams(("arbitrary",)),
        name="merge",
    )(y_fox, y_gdn, proj, proj, x2d, wf, wg, wo, n2, wr, br)


LANE_RANK = TOP_K


def _route_kernel(lg_ref, ri_ref, rw_ref, cnt_ref, carry):
    i = pl.program_id(0)
    tm = lg_ref.shape[0]

    @pl.when(i == 0)
    def _():
        carry[...] = jnp.zeros_like(carry)

    lane = lax.broadcasted_iota(jnp.int32, (tm, LANES), 1)
    lane_f = lane.astype(F32)
    v = jnp.where(lane < N_EXPERTS, lg_ref[...], -jnp.inf)
    vals, idxs, hots = [], [], []
    for _ in range(TOP_K):
        m = jnp.max(v, axis=1, keepdims=True)
        idx = jnp.min(jnp.where(v == m, lane_f, float(LANES)), axis=1, keepdims=True)
        hot = lane_f == idx
        vals.append(m)
        idxs.append(idx)
        hots.append(hot)
        v = jnp.where(hot, -jnp.inf, v)

    exps = [jnp.exp(val - vals[0]) for val in vals]
    den = exps[0] + exps[1] + exps[2] + exps[3]

    multi_hot = jnp.zeros((tm, LANES), F32)
    for hot in hots:
        multi_hot = multi_hot + hot.astype(F32)
    row = lax.broadcasted_iota(jnp.int32, (tm, tm), 0)
    col = lax.broadcasted_iota(jnp.int32, (tm, tm), 1)
    before = _dot((row > col).astype(BF16), multi_hot.astype(BF16)) + carry[0:1, :]
    total = before[tm - 1:tm, :] + multi_hot[tm - 1:tm, :]
    carry[...] = jnp.broadcast_to(total, carry.shape)
    cnt_ref[...] = jnp.broadcast_to(total, cnt_ref.shape)

    out_i = jnp.zeros((tm, LANES), jnp.int32)
    out_w = jnp.zeros((tm, LANES), F32)
    for k in range(TOP_K):
        rank = jnp.sum(jnp.where(hots[k], before, 0.0), axis=1, keepdims=True).astype(jnp.int32)
        out_i = jnp.where(lane == k, idxs[k].astype(jnp.int32), out_i)
        out_i = jnp.where(lane == LANE_RANK + k, rank, out_i)
        out_w = jnp.where(lane == k, exps[k] / den, out_w)
    ri_ref[...] = out_i
    rw_ref[...] = out_w


def _route(logits):
    T = logits.shape[0]
    tm = min(256, T)
    return pl.pallas_call(
        _route_kernel,
        grid=(T // tm,),
        in_specs=[pl.BlockSpec((tm, LANES), lambda i: (i, 0))],
        out_specs=[
            pl.BlockSpec((tm, LANES), lambda i: (i, 0)),
            pl.BlockSpec((tm, LANES), lambda i: (i, 0)),
            pl.BlockSpec((8, LANES), lambda i: (0, 0)),
        ],
        out_shape=[
            jax.ShapeDtypeStruct((T, LANES), jnp.int32),
            jax.ShapeDtypeStruct((T, LANES), F32),
            jax.ShapeDtypeStruct((8, LANES), F32),
        ],
        scratch_shapes=[pltpu.VMEM((8, LANES), F32)],
        compiler_params=_cparams(("arbitrary",)),
        name="route",
    )(logits)


DISPATCH_TOKENS = 256


def _dispatch_kernel(dest_ref, h_ref, xs_in_ref, xs_ref, sem):
    del xs_in_ref

    def row_copy(r, k):
        return pltpu.make_async_copy(
            h_ref.at[pl.ds(r, 1)], xs_ref.at[pl.ds(dest_ref[r * TOP_K + k], 1)], sem)

    for r in range(DISPATCH_TOKENS):
        for k in range(TOP_K):
            row_copy(r, k).start(priority=k % 2)
    for r in range(DISPATCH_TOKENS):
        for k in range(TOP_K):
            row_copy(r, k).wait()


def _dispatch(dest_flat, h2, xs_zero):
    T, D = h2.shape
    n = DISPATCH_TOKENS * TOP_K
    return pl.pallas_call(
        _dispatch_kernel,
        grid=(T // DISPATCH_TOKENS,),
        in_specs=[
            pl.BlockSpec((n,), lambda i: (i,), memory_space=pltpu.SMEM),
            pl.BlockSpec((DISPATCH_TOKENS, D), lambda i: (i, 0)),
            pl.BlockSpec(memory_space=pl.ANY),
        ],
        out_specs=pl.BlockSpec(memory_space=pl.ANY),
        out_shape=jax.ShapeDtypeStruct(xs_zero.shape, xs_zero.dtype),
        scratch_shapes=[pltpu.SemaphoreType.DMA(())],
        input_output_aliases={2: 0},
        compiler_params=pltpu.CompilerParams(
            dimension_semantics=("arbitrary",), has_side_effects=True),
        name="dispatch",
    )(dest_flat, h2, xs_zero)


def _expert_starts(be_ref, b):
    return jnp.logical_or(b == 0, be_ref[b] != be_ref[jnp.maximum(b - 1, 0)])


UP_TILE = 1024
DOWN_TILE = 1024
DEINTERLEAVE_W = 256


def _weight_tile_stream(be_ref, nxt_ref, w_hbm, stage, sem, width, on_arrival):
    j = pl.program_id(0)
    b = pl.program_id(1)

    def tile_copy(e, jj):
        col = pl.multiple_of(jj * width, width)
        return pltpu.make_async_copy(w_hbm.at[e, :, pl.ds(col, width)], stage, sem)

    @pl.when(jnp.logical_and(j == 0, b == 0))
    def _():
        tile_copy(be_ref[0], 0).start()

    tile_copy(be_ref[b], j).wait()
    on_arrival()
    nb = nxt_ref[b]
    more_experts = nb >= 0
    e_next = be_ref[jnp.where(more_experts, nb, 0)]
    j_next = jnp.where(more_experts, j, j + 1)

    @pl.when(jnp.logical_or(more_experts, j + 1 < pl.num_programs(0)))
    def _():
        tile_copy(e_next, j_next).start()


def _moe_up_kernel(be_ref, nu_ref, nxt_ref, x_ref, w_hbm, bg_ref, bl_ref, act_ref,
                   stage, wg_scr, wl_scr, sem):
    b = pl.program_id(1)
    active = b < nu_ref[0]

    def deinterleave():
        src = lax.broadcasted_iota(jnp.int32, (DEINTERLEAVE_W, DEINTERLEAVE_W), 0)
        dst = lax.broadcasted_iota(jnp.int32, (DEINTERLEAVE_W, DEINTERLEAVE_W), 1)
        half = DEINTERLEAVE_W // 2
        want = jnp.where(dst < half, 2 * dst, 2 * (dst - half) + 1)
        sel = (src == want).astype(BF16)
        for g in range(2 * UP_TILE // DEINTERLEAVE_W):
            chunk = stage[:, g * DEINTERLEAVE_W:(g + 1) * DEINTERLEAVE_W].astype(BF16)
            moved = _dot(chunk, sel).astype(BF16)
            wg_scr[:, g * half:(g + 1) * half] = moved[:, :half]
            wl_scr[:, g * half:(g + 1) * half] = moved[:, half:]

    @pl.when(jnp.logical_and(active, _expert_starts(be_ref, b)))
    def _():
        _weight_tile_stream(be_ref, nxt_ref, w_hbm, stage, sem, 2 * UP_TILE, deinterleave)

    @pl.when(active)
    def _():
        words = x_ref[...]
        x_hi = pltpu.bitcast(words & jnp.uint32(0xFFFF0000), F32).astype(BF16)
        x_lo = pltpu.bitcast(words << 16, F32).astype(BF16)
        x = jnp.concatenate([x_hi, x_lo], axis=1)
        gate = jnp.minimum(_dot(x, wg_scr[...]) + bg_ref[...], SWIGLU_LIMIT)
        lin = jnp.clip(_dot(x, wl_scr[...]) + bl_ref[...], -SWIGLU_LIMIT, SWIGLU_LIMIT)
        act = gate * jax.nn.sigmoid(SWIGLU_ALPHA * gate) * (lin + 1.0)
        act_ref[...] = act.astype(act_ref.dtype)

    @pl.when(jnp.logical_not(active))
    def _():
        act_ref[...] = jnp.zeros_like(act_ref)


def _moe_up(block_expert, n_used, next_start, xs, w_gate_up, b_gate, b_lin):
    P = xs.shape[0]
    D = w_gate_up.shape[1]
    dff = w_gate_up.shape[2] // 2
    tn = UP_TILE
    nb = P // MOE_ROWS

    def blk(b, nu):
        return jnp.minimum(b, nu[0] - 1)

    grid_spec = pltpu.PrefetchScalarGridSpec(
        num_scalar_prefetch=3,
        grid=(dff // tn, nb),
        in_specs=[
            pl.BlockSpec((MOE_ROWS, D // 2), lambda j, b, be, nu, nx: (blk(b, nu), 0)),
            pl.BlockSpec(memory_space=pl.ANY),
            pl.BlockSpec((None, 1, tn), lambda j, b, be, nu, nx: (be[blk(b, nu)], 0, j)),
            pl.BlockSpec((None, 1, tn), lambda j, b, be, nu, nx: (be[blk(b, nu)], 0, j)),
        ],
        out_specs=pl.BlockSpec((MOE_ROWS, tn), lambda j, b, be, nu, nx: (b, j)),
        scratch_shapes=[
            pltpu.VMEM((D, 2 * tn), F32),
            pltpu.VMEM((D, tn), BF16),
            pltpu.VMEM((D, tn), BF16),
            pltpu.SemaphoreType.DMA(()),
        ],
    )
    return pl.pallas_call(
        _moe_up_kernel,
        grid_spec=grid_spec,
        out_shape=jax.ShapeDtypeStruct((P, dff), BF16),
        compiler_params=_cparams(("arbitrary", "arbitrary")),
        name="moe_up",
    )(block_expert, n_used, next_start, xs, w_gate_up, b_gate, b_lin)


def _moe_down_kernel(be_ref, nu_ref, nxt_ref, a_ref, w_hbm, bd_ref, y_ref, stage, wd_scr, sem):
    b = pl.program_id(1)
    active = b < nu_ref[0]

    def to_bf16():
        wd_scr[...] = stage[...].astype(BF16)

    @pl.when(jnp.logical_and(active, _expert_starts(be_ref, b)))
    def _():
        _weight_tile_stream(be_ref, nxt_ref, w_hbm, stage, sem, DOWN_TILE, to_bf16)

    @pl.when(active)
    def _():
        y_ref[...] = _dot(a_ref[...], wd_scr[...]) + bd_ref[...]

    @pl.when(jnp.logical_not(active))
    def _():
        y_ref[...] = jnp.zeros_like(y_ref)


def _moe_down(block_expert, n_used, next_start, act, w_down, b_down):
    P, dff = act.shape
    D = w_down.shape[2]
    tn = DOWN_TILE
    nb = P // MOE_ROWS

    def blk(b, nu):
        return jnp.minimum(b, nu[0] - 1)

    grid_spec = pltpu.PrefetchScalarGridSpec(
        num_scalar_prefetch=3,
        grid=(D // tn, nb),
        in_specs=[
            pl.BlockSpec((MOE_ROWS, dff), lambda j, b, be, nu, nx: (blk(b, nu), 0)),
            pl.BlockSpec(memory_space=pl.ANY),
            pl.BlockSpec((None, 1, tn), lambda j, b, be, nu, nx: (be[blk(b, nu)], 0, j)),
        ],
        out_specs=pl.BlockSpec((MOE_ROWS, tn), lambda j, b, be, nu, nx: (b, j)),
        scratch_shapes=[
            pltpu.VMEM((dff, tn), F32),
            pltpu.VMEM((dff, tn), BF16),
            pltpu.SemaphoreType.DMA(()),
        ],
    )
    return pl.pallas_call(
        _moe_down_kernel,
        grid_spec=grid_spec,
        out_shape=jax.ShapeDtypeStruct((P, D), F32),
        compiler_params=_cparams(("arbitrary", "arbitrary")),
        name="moe_down",
    )(block_expert, n_used, next_start, act, w_down, b_down)


COMBINE_TOKENS = 256


def _combine_kernel(dest_ref, y_ref, x2_ref, rw_ref, o_ref, buf, sem):
    def row_copy(r, k):
        return pltpu.make_async_copy(
            y_ref.at[pl.ds(dest_ref[r * TOP_K + k], 1)], buf.at[k, pl.ds(r, 1)], sem)

    for r in range(COMBINE_TOKENS):
        for k in range(TOP_K):
            row_copy(r, k).start(priority=k % 2)
    for r in range(COMBINE_TOKENS):
        for k in range(TOP_K):
            row_copy(r, k).wait()
    acc = x2_ref[...]
    for k in range(TOP_K):
        acc = acc + rw_ref[:, k:k + 1] * buf[k]
    o_ref[...] = acc


def _combine(dest_flat, y, x2, route_w):
    T, D = x2.shape
    tc = COMBINE_TOKENS
    return pl.pallas_call(
        _combine_kernel,
        grid=(T // tc,),
        in_specs=[
            pl.BlockSpec((tc * TOP_K,), lambda i: (i,), memory_space=pltpu.SMEM),
            pl.BlockSpec(memory_space=pl.ANY),
            pl.BlockSpec((tc, D), lambda i: (i, 0)),
            pl.BlockSpec((tc, LANES), lambda i: (i, 0)),
        ],
        out_specs=pl.BlockSpec((tc, D), lambda i: (i, 0)),
        out_shape=jax.ShapeDtypeStruct((T, D), F32),
        scratch_shapes=[pltpu.VMEM((TOP_K, tc, D), F32), pltpu.SemaphoreType.DMA(())],
        compiler_params=_cparams(("arbitrary",)),
        name="combine",
    )(dest_flat, y, x2, route_w)


def _pad_lanes(v, offset):
    return jnp.zeros((LANES,), F32).at[offset:offset + v.shape[0]].set(v.astype(F32))


def _layer(x, norm1_w, w_in, b_fox_f, fox_q_norm_w, fox_k_norm_w, gdn_conv_w, gdn_A_log,
           gdn_dt_bias, gdn_out_norm_w, w_up_fox, w_up_gdn, w_o, norm2_w, w_router, b_router,
           w_gate_up, b_gate_up, w_down, b_down):
    B, S, D = x.shape
    T = B * S
    x2d = x.reshape(T, D)

    o_f = 3 * HEADS_W
    o_gqkv = o_f + N_HEADS
    o_gb = o_gqkv + 4 * HEADS_W
    o_gate = o_gb + 2 * N_HEADS
    w_big = jnp.concatenate(
        [w_in[:, o_gate:], w_in[:, :o_f], w_in[:, o_gqkv:o_gb]], axis=1).astype(BF16)
    w_small = jnp.concatenate(
        [w_in[:, o_f:o_gqkv], w_in[:, o_gb:o_gate],
         jnp.zeros((D, LANES - 3 * N_HEADS), w_in.dtype)], axis=1).astype(BF16)

    proj, small, v_t = _in_proj(x2d, norm1_w.reshape(1, D), w_big, w_small,
                                fox_q_norm_w.reshape(1, HEAD_DIM), fox_k_norm_w.reshape(1, HEAD_DIM))

    gate_params = jnp.zeros((8, LANES), F32)
    gate_params = gate_params.at[0].set(_pad_lanes(b_fox_f, LANE_C) + _pad_lanes(gdn_dt_bias, LANE_GC))
    gate_params = gate_params.at[1].set(_pad_lanes(gdn_A_log, LANE_GC))
    gates = _gates(small, gate_params, S)
    gates_t = gates[:, :32].T
    c_bcast = jnp.broadcast_to(
        (gates_t[LANE_C:LANE_C + N_HEADS] * LOG2E)[:, :, None], (N_HEADS, T, LANES))

    y_fox = _fox(proj, v_t, c_bcast, B, S)

    u, w, qd, at, kdt, gl = _gdn_prep(proj, gdn_conv_w.astype(F32), gates, gates_t, S)
    y_gdn = _gdn_scan(u, w, qd, at, kdt, gl, proj, gdn_out_norm_w.reshape(1, HEAD_DIM), B, S)

    w_router_p = jnp.zeros((D, LANES), F32).at[:, :N_EXPERTS].set(w_router.astype(F32))
    w_router_hi = w_router_p.astype(BF16)
    w_router_lo = (w_router_p - w_router_hi.astype(F32)).astype(BF16)
    w_router_split = jnp.concatenate([w_router_hi, w_router_lo], axis=1)
    b_router_p = _pad_lanes(b_router, 0).reshape(1, LANES)
    x2, h2, logits = _merge(y_fox, y_gdn, proj, x2d, w_up_fox.astype(BF16), w_up_gdn.astype(BF16),
                            w_o.astype(BF16), norm2_w.reshape(1, D), w_router_split, b_router_p)

    route_i, route_w, counts = _route(logits)
    expert = route_i[:, :TOP_K]
    rank = route_i[:, LANE_RANK:LANE_RANK + TOP_K]
    counts = counts[0, :N_EXPERTS].astype(jnp.int32)
    padded = (counts + MOE_ROWS - 1) // MOE_ROWS * MOE_ROWS
    pad_end = jnp.cumsum(padded)
    pad_start = pad_end - padded
    dest_flat = (pad_start[expert] + rank).reshape(T * TOP_K).astype(jnp.int32)
    n_blocks = -(-T * TOP_K // MOE_ROWS) + N_EXPERTS
    n_used = (pad_end[-1:] // MOE_ROWS).astype(jnp.int32)
    block_start = jnp.arange(n_blocks, dtype=jnp.int32) * MOE_ROWS
    block_expert = jnp.minimum(
        jnp.sum((pad_end[None, :] <= block_start[:, None]).astype(jnp.int32), axis=1),
        N_EXPERTS - 1).astype(jnp.int32)

    xs = _dispatch(dest_flat, h2, jnp.zeros((n_blocks * MOE_ROWS, h2.shape[1]), h2.dtype))

    dff = w_down.shape[1]
    b_gu = b_gate_up.reshape(N_EXPERTS, 1, dff, 2).astype(F32)
    after = (pad_end[block_expert] // MOE_ROWS).astype(jnp.int32)
    next_start = jnp.where(after < n_used[0], after, -1).astype(jnp.int32)
    act = _moe_up(block_expert, n_used, next_start, xs, w_gate_up.astype(F32),
                  b_gu[..., 0], b_gu[..., 1])
    y = _moe_down(block_expert, n_used, next_start, act, w_down.astype(F32),
                  b_down.reshape(N_EXPERTS, 1, D).astype(F32))

    out = _combine(dest_flat, y, x2, route_w)
    return out.reshape(B, S, D)


def kernel(x, norm1_w, w_in, b_fox_f, fox_q_norm_w, fox_k_norm_w, gdn_conv_w, gdn_A_log, gdn_dt_bias, gdn_out_norm_w, w_up_fox, w_up_gdn, w_o, norm2_w, w_router, b_router, w_gate_up, b_gate_up, w_down, b_down):
    depth = norm1_w.shape[0]
    for l in range(depth):
        x = _layer(x, norm1_w[l], w_in[l], b_fox_f[l], fox_q_norm_w[l], fox_k_norm_w[l],
                   gdn_conv_w[l], gdn_A_log[l], gdn_dt_bias[l], gdn_out_norm_w[l], w_up_fox[l],
                   w_up_gdn[l], w_o[l], norm2_w[l], w_router[l], b_router[l], w_gate_up[l],
                   b_gate_up[l], w_down[l], b_down[l])
    return x
```

```python
import functools

import jax
import jax.numpy as jnp
from jax import lax
from jax.experimental import pallas as pl
from jax.experimental.pallas import tpu as pltpu

F32 = jnp.float32
BF16 = jnp.bfloat16

N_HEADS = 8
HEAD_DIM = 128
HEADS_W = N_HEADS * HEAD_DIM
GDN_CHUNK = 64
CONV_WIDTH = 4
N_EXPERTS = 32
TOP_K = 4
SWIGLU_LIMIT = 7.0
SWIGLU_ALPHA = 1.702
RMS_EPS = 1e-6
LANES = 128
NEG_BIG = -1e30
LOG2E = 1.4426950408889634

VMEM_LIMIT = 56 * 1024 * 1024

MOE_ROWS = 512


def _cparams(sem):
    return pltpu.CompilerParams(dimension_semantics=sem, vmem_limit_bytes=VMEM_LIMIT)


def _nt_dot(a, b):
    return lax.dot_general(a, b, (((1,), (1,)), ((), ())), preferred_element_type=F32)


def _dot(a, b):
    return jnp.dot(a, b, preferred_element_type=F32)


def _dot_exact(a, b):
    return jnp.dot(a, b, preferred_element_type=F32, precision=lax.Precision.HIGHEST)


def _silu(x):
    return x * jax.nn.sigmoid(x)


BLK_GATE_A, BLK_GATE_B = 0, 2
BLK_FQ, BLK_FK, BLK_FV = 4, 5, 6
BLK_GQ, BLK_GK, BLK_GV, BLK_GZ = 7, 8, 9, 10
CHUNK_SHIFT = GDN_CHUNK.bit_length() - 1


def _in_proj_kernel(x_ref, n1_ref, w_ref, ws_ref, qn_ref, kn_ref, proj_ref, small_ref, vt_ref,
                    h_scr):
    j = pl.program_id(1)

    @pl.when(j == 0)
    def _():
        x = x_ref[...]
        ms = jnp.mean(x * x, axis=-1, keepdims=True)
        h = (x * lax.rsqrt(ms + RMS_EPS) * n1_ref[...]).astype(BF16)
        h_scr[...] = h
        small_ref[...] = _dot(h, ws_ref[...])

    acc = _dot(h_scr[...], w_ref[...])

    def head_norm(nw_ref, scale):
        for hh in range(N_HEADS):
            a = acc[:, hh * HEAD_DIM:(hh + 1) * HEAD_DIM]
            ms = jnp.mean(a * a, axis=-1, keepdims=True)
            y = a * lax.rsqrt(ms + RMS_EPS) * nw_ref[...] * scale
            proj_ref[:, hh * HEAD_DIM:(hh + 1) * HEAD_DIM] = y.astype(BF16)

    @pl.when(j == BLK_FQ)
    def _():
        head_norm(qn_ref, HEAD_DIM ** -0.5 * LOG2E)

    @pl.when(j == BLK_FK)
    def _():
        head_norm(kn_ref, 1.0)

    @pl.when(j == BLK_FV)
    def _():
        vt_ref[...] = acc.T.astype(BF16)

    @pl.when(jnp.logical_and(j != BLK_FQ, j != BLK_FK))
    def _():
        proj_ref[...] = acc.astype(BF16)


def _in_proj(x2d, n1, w_big, w_small, qn, kn):
    T, D = x2d.shape
    n_cols = w_big.shape[1]
    tm = min(1024, T)
    tn = HEADS_W
    return pl.pallas_call(
        _in_proj_kernel,
        grid=(T // tm, n_cols // tn),
        in_specs=[
            pl.BlockSpec((tm, D), lambda i, j: (i, 0)),
            pl.BlockSpec((1, D), lambda i, j: (0, 0)),
            pl.BlockSpec((D, tn), lambda i, j: (0, j)),
            pl.BlockSpec((D, LANES), lambda i, j: (0, 0)),
            pl.BlockSpec((1, HEAD_DIM), lambda i, j: (0, 0)),
            pl.BlockSpec((1, HEAD_DIM), lambda i, j: (0, 0)),
        ],
        out_specs=[
            pl.BlockSpec((tm, tn), lambda i, j: (i, j)),
            pl.BlockSpec((tm, LANES), lambda i, j: (i, 0)),
            pl.BlockSpec((HEADS_W, tm), lambda i, j: (0, i)),
        ],
        out_shape=[
            jax.ShapeDtypeStruct((T, n_cols), BF16),
            jax.ShapeDtypeStruct((T, LANES), F32),
            jax.ShapeDtypeStruct((HEADS_W, T), BF16),
        ],
        scratch_shapes=[pltpu.VMEM((tm, D), BF16)],
        compiler_params=_cparams(("arbitrary", "arbitrary")),
        name="in_proj",
    )(x2d, n1, w_big, w_small, qn, kn)


LANE_C, LANE_BETA, LANE_GC = 0, N_HEADS, 2 * N_HEADS


def _gates_kernel(s_ref, p_ref, g_ref, carry, *, tiles_per_seq):
    i = pl.program_id(0)
    tg = s_ref.shape[0]

    @pl.when(i % tiles_per_seq == 0)
    def _():
        carry[...] = jnp.zeros_like(carry)

    z = s_ref[...] + p_ref[0:1, :]
    soft = jnp.log1p(jnp.exp(-jnp.abs(z)))
    log_sig = jnp.minimum(z, 0.0) - soft
    softplus = jnp.maximum(z, 0.0) + soft
    beta = jax.nn.sigmoid(z)
    g = -jnp.exp(p_ref[1:2, :]) * softplus

    row = lax.broadcasted_iota(jnp.int32, (tg, tg), 0)
    col = lax.broadcasted_iota(jnp.int32, (tg, tg), 1)
    tri = row >= col
    tri_chunk = jnp.logical_and(tri, (row >> CHUNK_SHIFT) == (col >> CHUNK_SHIFT))
    c = _dot_exact(tri.astype(F32), log_sig) + carry[...]
    gc = _dot_exact(tri_chunk.astype(F32), g)
    carry[...] = c[tg - 1:tg, :]

    lane = lax.broadcasted_iota(jnp.int32, (tg, LANES), 1)
    g_ref[...] = jnp.where(lane < LANE_BETA, c,
                           jnp.where(lane < LANE_GC, beta,
                                     jnp.where(lane < LANE_GC + N_HEADS, gc, 0.0)))


def _gates(small, params, seq_len):
    T = small.shape[0]
    tg = min(256, seq_len)
    return pl.pallas_call(
        functools.partial(_gates_kernel, tiles_per_seq=seq_len // tg),
        grid=(T // tg,),
        in_specs=[
            pl.BlockSpec((tg, LANES), lambda i: (i, 0)),
            pl.BlockSpec((8, LANES), lambda i: (0, 0)),
        ],
        out_specs=pl.BlockSpec((tg, LANES), lambda i: (i, 0)),
        out_shape=jax.ShapeDtypeStruct((T, LANES), F32),
        scratch_shapes=[pltpu.VMEM((1, LANES), F32)],
        compiler_params=_cparams(("arbitrary",)),
        name="gates",
    )(small, params)


FOX_TILE = 1024


def _fox_kernel(qi_ref, kj_ref, q_ref, k_ref, vt_ref, cb_ref, o_ref, m_scr, l_scr, acc_scr):
    p = pl.program_id(2)
    i = qi_ref[p]
    j = kj_ref[p]
    tq, tk = q_ref.shape[0], k_ref.shape[0]

    @pl.when(j == 0)
    def _():
        m_scr[...] = jnp.full_like(m_scr, NEG_BIG)
        l_scr[...] = jnp.zeros_like(l_scr)
        acc_scr[...] = jnp.zeros_like(acc_scr)

    def step(k0, kn, q0, qn, masked):
        st = _nt_dot(k_ref[k0:k0 + kn, :], q_ref[q0:q0 + qn, :])
        st = st - jnp.concatenate([cb_ref[k0:k0 + kn, :]] * (qn // LANES), axis=1)
        if masked:
            key = lax.broadcasted_iota(jnp.int32, (kn, qn), 0) + k0
            qry = lax.broadcasted_iota(jnp.int32, (kn, qn), 1) + q0
            st = jnp.where(qry >= key, st, -jnp.inf)
        m_old = m_scr[:, q0:q0 + qn]
        m_new = jnp.maximum(m_old, jnp.max(st, axis=0, keepdims=True))
        alpha = jnp.exp2(m_old - m_new)
        pt = jnp.exp2(st - m_new)
        l_scr[:, q0:q0 + qn] = alpha * l_scr[:, q0:q0 + qn] + jnp.sum(pt, axis=0, keepdims=True)
        acc_scr[:, q0:q0 + qn] = (alpha * acc_scr[:, q0:q0 + qn]
                                  + _dot(vt_ref[:, k0:k0 + kn], pt.astype(BF16)))
        m_scr[:, q0:q0 + qn] = m_new

    @pl.when(j < i)
    def _():
        step(0, tk, 0, tq, False)

    @pl.when(j == i)
    def _():
        half = tk // 2
        step(0, half, 0, tq, True)
        step(half, tk - half, half, tq - half, True)
        o_ref[...] = (acc_scr[...] / l_scr[...]).T.astype(o_ref.dtype)


def _fox(proj, v_t, c_bcast, batch, seq_len):
    T = proj.shape[0]
    tq = min(FOX_TILE, seq_len)
    nq = seq_len // tq
    pairs = [(i, j) for i in range(nq) for j in range(i + 1)]
    qi = jnp.asarray([p[0] for p in pairs], jnp.int32)
    kj = jnp.asarray([p[1] for p in pairs], jnp.int32)
    q_off, k_off = BLK_FQ * N_HEADS, BLK_FK * N_HEADS
    grid_spec = pltpu.PrefetchScalarGridSpec(
        num_scalar_prefetch=2,
        grid=(batch, N_HEADS, len(pairs)),
        in_specs=[
            pl.BlockSpec((tq, HEAD_DIM), lambda b, h, p, qi, kj: (b * nq + qi[p], q_off + h)),
            pl.BlockSpec((tq, HEAD_DIM), lambda b, h, p, qi, kj: (b * nq + kj[p], k_off + h)),
            pl.BlockSpec((HEAD_DIM, tq), lambda b, h, p, qi, kj: (h, b * nq + kj[p])),
            pl.BlockSpec((None, tq, LANES), lambda b, h, p, qi, kj: (h, b * nq + kj[p], 0)),
        ],
        out_specs=pl.BlockSpec((tq, HEAD_DIM), lambda b, h, p, qi, kj: (b * nq + qi[p], h)),
        scratch_shapes=[
            pltpu.VMEM((1, tq), F32),
            pltpu.VMEM((1, tq), F32),
            pltpu.VMEM((HEAD_DIM, tq), F32),
        ],
    )
    return pl.pallas_call(
        _fox_kernel,
        grid_spec=grid_spec,
        out_shape=jax.ShapeDtypeStruct((T, HEADS_W), BF16),
        compiler_params=_cparams(("arbitrary", "arbitrary", "arbitrary")),
        name="fox",
    )(qi, kj, proj, proj, v_t, c_bcast)


GDN_TILE = 2 * GDN_CHUNK
HALO = 16


def _gdn_prep_kernel(q_ref, k_ref, v_ref, hq_ref, hk_ref, hv_ref, cw_ref, g_ref, gt_ref,
                     u_ref, w_ref, qd_ref, at_ref, kdt_ref, gl_ref, *, tiles_per_seq):
    i = pl.program_id(0)
    R = GDN_TILE
    first = (i % tiles_per_seq) == 0
    heads = range(N_HEADS)

    def hslice(x, hh):
        return x[:, hh * HEAD_DIM:(hh + 1) * HEAD_DIM]

    dst_t = lax.broadcasted_iota(jnp.int32, (R, 2 * R), 0)
    src_t = lax.broadcasted_iota(jnp.int32, (R, 2 * R), 1)
    src_ok = jnp.logical_or(src_t >= HALO, jnp.logical_not(first))
    shifts = [jnp.logical_and(src_t == dst_t + HALO - back, src_ok).astype(BF16)
              for back in range(CONV_WIDTH - 1, 0, -1)]
    zero_rows = jnp.zeros((2 * R - HALO - R, HEADS_W), BF16)

    def conv_silu(grp, m_ref, h_ref):
        main = m_ref[...]
        stacked = jnp.concatenate([h_ref[...], main, zero_rows], axis=0)
        cw = cw_ref[:, grp * HEADS_W:(grp + 1) * HEADS_W]
        acc = main.astype(F32) * cw[CONV_WIDTH - 1:CONV_WIDTH, :]
        for tap in range(CONV_WIDTH - 1):
            acc = acc + _dot(shifts[tap], stacked) * cw[tap:tap + 1, :]
        return _silu(acc)

    q_all = conv_silu(0, q_ref, hq_ref)
    k_all = conv_silu(1, k_ref, hk_ref)
    v_all = conv_silu(2, v_ref, hv_ref)

    def l2n(x):
        return x * lax.rsqrt(jnp.sum(x * x, axis=-1, keepdims=True) + RMS_EPS)

    qn = [l2n(hslice(q_all, hh)) * (HEAD_DIM ** -0.5) for hh in heads]
    kn = [l2n(hslice(k_all, hh)) for hh in heads]
    qb = [x.astype(BF16) for x in qn]
    kb = [x.astype(BF16) for x in kn]
    vb = [hslice(v_all, hh).astype(BF16) for hh in heads]

    row = lax.broadcasted_iota(jnp.int32, (R, R), 0)
    col = lax.broadcasted_iota(jnp.int32, (R, R), 1)
    same = (row >> CHUNK_SHIFT) == (col >> CHUNK_SHIFT)
    incl = jnp.logical_and(same, row >= col)
    strict = jnp.logical_and(same, row > col)
    eye = (row == col).astype(F32)
    top = row < GDN_CHUNK

    gcol = [jnp.broadcast_to(g_ref[:, LANE_GC + hh:LANE_GC + hh + 1], (R, R)) for hh in heads]
    bcol = [jnp.broadcast_to(g_ref[:, LANE_BETA + hh:LANE_BETA + hh + 1], (R, R)) for hh in heads]
    grow = [gt_ref[LANE_GC + hh:LANE_GC + hh + 1, :] for hh in heads]
    brow = [gt_ref[LANE_BETA + hh:LANE_BETA + hh + 1, :] for hh in heads]

    decay = [jnp.where(incl, jnp.exp(jnp.minimum(gcol[hh] - grow[hh], 0.0)), 0.0) for hh in heads]
    kk = [_nt_dot(kb[hh], kb[hh]) for hh in heads]
    qk = [_nt_dot(qb[hh], kb[hh]) for hh in heads]
    lmat = [jnp.where(strict, kk[hh] * bcol[hh] * decay[hh], 0.0) for hh in heads]

    inv = [eye - lm for lm in lmat]
    power = lmat
    for _ in range(GDN_CHUNK.bit_length() - 2):
        pb = [pw.astype(BF16) for pw in power]
        power = [_dot(x, x) for x in pb]
        inv = [iv + _dot(iv.astype(BF16), pw.astype(BF16)) for iv, pw in zip(inv, power)]

    sub = lax.broadcasted_iota(jnp.int32, (8, R), 0)
    for hh in heads:
        cs = hh * HEAD_DIM
        u = _dot((inv[hh] * brow[hh]).astype(BF16), vb[hh])
        w = _dot((inv[hh] * (brow[hh] * jnp.exp(grow[hh]))).astype(BF16), kb[hh])

        g_last0 = gcol[hh][GDN_CHUNK - 1:GDN_CHUNK, :]
        g_last1 = gcol[hh][R - 1:R, :]
        g_last = jnp.where(top, g_last0, g_last1)
        q_dec = qn[hh] * jnp.exp(gcol[hh])
        k_dec = kn[hh] * jnp.exp(g_last - gcol[hh])

        attn = qk[hh] * decay[hh]
        attn_own = jnp.where(top, attn, pltpu.roll(attn, GDN_CHUNK, 1))
        attn_own = jnp.where(col < GDN_CHUNK, attn_own, 0.0)

        u_ref[:, cs:cs + HEAD_DIM] = u.astype(BF16)
        w_ref[:, cs:cs + HEAD_DIM] = w.astype(BF16)
        qd_ref[:, cs:cs + HEAD_DIM] = q_dec.astype(BF16)
        at_ref[:, cs:cs + HEAD_DIM] = attn_own.astype(BF16)
        kdt_ref[cs:cs + HEAD_DIM, :] = k_dec.T.astype(BF16)
        gl_ref[:, cs:cs + HEAD_DIM] = jnp.where(
            sub == 0, jnp.exp(g_last0), jnp.where(sub == 1, jnp.exp(g_last1), 0.0))


def _gdn_prep(proj, conv_w, gates, gates_t, seq_len):
    T = proj.shape[0]
    R = GDN_TILE
    n_tiles = T // R
    q_blk, k_blk, v_blk = BLK_GQ, BLK_GK, BLK_GV
    per_tile = R // HALO

    def main_spec(cb):
        return pl.BlockSpec((R, HEADS_W), lambda i: (i, cb))

    def halo_spec(cb):
        return pl.BlockSpec((HALO, HEADS_W), lambda i: (jnp.maximum(i * per_tile - 1, 0), cb))

    wide = pl.BlockSpec((R, HEADS_W), lambda i: (i, 0))
    return pl.pallas_call(
        functools.partial(_gdn_prep_kernel, tiles_per_seq=seq_len // R),
        grid=(n_tiles,),
        in_specs=[
            main_spec(q_blk), main_spec(k_blk), main_spec(v_blk),
            halo_spec(q_blk), halo_spec(k_blk), halo_spec(v_blk),
            pl.BlockSpec((CONV_WIDTH, 3 * HEADS_W), lambda i: (0, 0)),
            pl.BlockSpec((R, LANES), lambda i: (i, 0)),
            pl.BlockSpec((32, R), lambda i: (0, i)),
        ],
        out_specs=[
            wide, wide, wide, wide,
            pl.BlockSpec((HEADS_W, R), lambda i: (0, i)),
            pl.BlockSpec((8, HEADS_W), lambda i: (i, 0)),
        ],
        out_shape=[
            jax.ShapeDtypeStruct((T, HEADS_W), BF16),
            jax.ShapeDtypeStruct((T, HEADS_W), BF16),
            jax.ShapeDtypeStruct((T, HEADS_W), BF16),
            jax.ShapeDtypeStruct((T, HEADS_W), BF16),
            jax.ShapeDtypeStruct((HEADS_W, T), BF16),
            jax.ShapeDtypeStruct((n_tiles * 8, HEADS_W), F32),
        ],
        compiler_params=_cparams(("arbitrary",)),
        name="gdn_prep",
    )(proj, proj, proj, proj, proj, proj, conv_w, gates, gates_t)


def _gdn_scan_kernel(u_ref, w_ref, qd_ref, at_ref, kdt_ref, gl_ref, z_ref, nw_ref, y_ref, state):
    s = pl.program_id(1)
    rows = u_ref.shape[0]

    @pl.when(s == 0)
    def _():
        state[...] = jnp.zeros_like(state)

    zeros_half = jnp.zeros((GDN_CHUNK, HEAD_DIM), BF16)
    heads = range(N_HEADS)

    def blk(ref, r0, hh):
        return ref[r0:r0 + GDN_CHUNK, hh * HEAD_DIM:(hh + 1) * HEAD_DIM]

    st = [state[hh] for hh in heads]
    for c in range(rows // GDN_CHUNK):
        r0 = c * GDN_CHUNK
        tile, half = divmod(c, 2)
        t0 = tile * GDN_TILE
        sb = [x.astype(BF16) for x in st]
        wq = [_dot(jnp.concatenate([blk(w_ref, r0, hh), blk(qd_ref, r0, hh)], axis=0), sb[hh])
              for hh in heads]
        v_new = [(blk(u_ref, r0, hh).astype(F32) - wq[hh][:GDN_CHUNK]).astype(BF16) for hh in heads]
        v_lo = [jnp.concatenate([x, zeros_half], axis=0) for x in v_new]
        v_own = v_lo if half == 0 else [jnp.concatenate([zeros_half, x], axis=0) for x in v_new]
        o = [wq[hh][GDN_CHUNK:] + _dot(blk(at_ref, r0, hh), v_lo[hh]) for hh in heads]
        st = [st[hh] * gl_ref[tile * 8 + half:tile * 8 + half + 1, hh * HEAD_DIM:(hh + 1) * HEAD_DIM]
              + _dot(kdt_ref[hh * HEAD_DIM:(hh + 1) * HEAD_DIM, t0:t0 + GDN_TILE], v_own[hh])
              for hh in heads]
        for hh in heads:
            ms = jnp.mean(o[hh] * o[hh], axis=-1, keepdims=True)
            z = blk(z_ref, r0, hh).astype(F32)
            y = o[hh] * lax.rsqrt(ms + RMS_EPS) * nw_ref[...] * _silu(z)
            y_ref[r0:r0 + GDN_CHUNK, hh * HEAD_DIM:(hh + 1) * HEAD_DIM] = y.astype(y_ref.dtype)
    for hh in heads:
        state[hh] = st[hh]


def _gdn_scan(u, w, qd, at, kdt, gl, proj, out_norm_w, batch, seq_len):
    T = u.shape[0]
    rows = min(256, seq_len)
    ns = seq_len // rows
    z_blk = BLK_GZ
    wide = pl.BlockSpec((rows, HEADS_W), lambda b, s: (b * ns + s, 0))
    gl_rows = rows // GDN_TILE * 8
    return pl.pallas_call(
        _gdn_scan_kernel,
        grid=(batch, ns),
        in_specs=[
            wide, wide, wide, wide,
            pl.BlockSpec((HEADS_W, rows), lambda b, s: (0, b * ns + s)),
            pl.BlockSpec((gl_rows, HEADS_W), lambda b, s: (b * ns + s, 0)),
            pl.BlockSpec((rows, HEADS_W), lambda b, s: (b * ns + s, z_blk)),
            pl.BlockSpec((1, HEAD_DIM), lambda b, s: (0, 0)),
        ],
        out_specs=wide,
        out_shape=jax.ShapeDtypeStruct((T, HEADS_W), BF16),
        scratch_shapes=[pltpu.VMEM((N_HEADS, HEAD_DIM, HEAD_DIM), F32)],
        compiler_params=_cparams(("arbitrary", "arbitrary")),
        name="gdn_scan",
    )(u, w, qd, at, kdt, gl, proj, out_norm_w)


def _merge_kernel(yf_ref, yg_ref, ga_ref, gb_ref, x_ref, wf_ref, wg_ref, wo_ref, n2_ref,
                  wr_ref, br_ref, x2_ref, h2_ref, lg_ref):
    a = _dot(yf_ref[...], wf_ref[...])
    b = _dot(yg_ref[...], wg_ref[...])
    merged = (jax.nn.sigmoid(ga_ref[...].astype(F32)) * a
              + jax.nn.sigmoid(gb_ref[...].astype(F32)) * b)
    x2 = x_ref[...] + _dot(merged.astype(BF16), wo_ref[...])
    x2_ref[...] = x2
    ms = jnp.mean(x2 * x2, axis=-1, keepdims=True)
    h2 = x2 * lax.rsqrt(ms + RMS_EPS) * n2_ref[...]
    h_hi = h2.astype(BF16)
    h_hi32 = h_hi.astype(F32)
    h_lo = (h2 - h_hi32).astype(BF16)

    hw = _dot(h_hi, wr_ref[...])
    lg_ref[...] = (hw[:, :LANES] + hw[:, LANES:] + _dot(h_lo, wr_ref[:, :LANES])) + br_ref[...]

    half = h2.shape[1] // 2
    bits = pltpu.bitcast(h_hi32, jnp.uint32)
    h2_ref[...] = (bits[:, :half] & jnp.uint32(0xFFFF0000)) | (bits[:, half:] >> 16)


def _merge(y_fox, y_gdn, proj, x2d, wf, wg, wo, n2, wr, br):
    T, D = x2d.shape
    tm = min(256, T)
    ga_blk, gb_blk = BLK_GATE_A * HEADS_W // D, BLK_GATE_B * HEADS_W // D
    const = lambda shape: pl.BlockSpec(shape, lambda i: (0, 0), pipeline_mode=pl.Buffered(1))
    return pl.pallas_call(
        _merge_kernel,
        grid=(T // tm,),
        in_specs=[
            pl.BlockSpec((tm, HEADS_W), lambda i: (i, 0)),
            pl.BlockSpec((tm, HEADS_W), lambda i: (i, 0)),
            pl.BlockSpec((tm, D), lambda i: (i, ga_blk)),
            pl.BlockSpec((tm, D), lambda i: (i, gb_blk)),
            pl.BlockSpec((tm, D), lambda i: (i, 0)),
            const((HEADS_W, D)), const((HEADS_W, D)), const((D, D)), const((1, D)),
            const((D, 2 * LANES)), const((1, LANES)),
        ],
        out_specs=[
            pl.BlockSpec((tm, D), lambda i: (i, 0)),
            pl.BlockSpec((tm, D // 2), lambda i: (i, 0)),
            pl.BlockSpec((tm, LANES), lambda i: (i, 0)),
        ],
        out_shape=[
            jax.ShapeDtypeStruct((T, D), F32),
            jax.ShapeDtypeStruct((T, D // 2), jnp.uint32),
            jax.ShapeDtypeStruct((T, LANES), F32),
        ],
        compiler_params=_cparams(("arbitrary",)),
        name="merge",
    )(y_fox, y_gdn, proj, proj, x2d, wf, wg, wo, n2, wr, br)


LANE_RANK = TOP_K


def _route_kernel(lg_ref, ri_ref, rw_ref, cnt_ref, carry):
    i = pl.program_id(0)
    tm = lg_ref.shape[0]

    @pl.when(i == 0)
    def _():
        carry[...] = jnp.zeros_like(carry)

    lane = lax.broadcasted_iota(jnp.int32, (tm, LANES), 1)
    lane_f = lane.astype(F32)
    v = jnp.where(lane < N_EXPERTS, lg_ref[...], -jnp.inf)
    vals, idxs, hots = [], [], []
    for _ in range(TOP_K):
        m = jnp.max(v, axis=1, keepdims=True)
        idx = jnp.min(jnp.where(v == m, lane_f, float(LANES)), axis=1, keepdims=True)
        hot = lane_f == idx
        vals.append(m)
        idxs.append(idx)
        hots.append(hot)
        v = jnp.where(hot, -jnp.inf, v)

    exps = [jnp.exp(val - vals[0]) for val in vals]
    den = exps[0] + exps[1] + exps[2] + exps[3]

    multi_hot = jnp.zeros((tm, LANES), F32)
    for hot in hots:
        multi_hot = multi_hot + hot.astype(F32)
    row = lax.broadcasted_iota(jnp.int32, (tm, tm), 0)
    col = lax.broadcasted_iota(jnp.int32, (tm, tm), 1)
    before = _dot((row > col).astype(BF16), multi_hot.astype(BF16)) + carry[0:1, :]
    total = before[tm - 1:tm, :] + multi_hot[tm - 1:tm, :]
    carry[...] = jnp.broadcast_to(total, carry.shape)
    cnt_ref[...] = jnp.broadcast_to(total, cnt_ref.shape)

    out_i = jnp.zeros((tm, LANES), jnp.int32)
    out_w = jnp.zeros((tm, LANES), F32)
    for k in range(TOP_K):
        rank = jnp.sum(jnp.where(hots[k], before, 0.0), axis=1, keepdims=True).astype(jnp.int32)
        out_i = jnp.where(lane == k, idxs[k].astype(jnp.int32), out_i)
        out_i = jnp.where(lane == LANE_RANK + k, rank, out_i)
        out_w = jnp.where(lane == k, exps[k] / den, out_w)
    ri_ref[...] = out_i
    rw_ref[...] = out_w


def _route(logits):
    T = logits.shape[0]
    tm = min(256, T)
    return pl.pallas_call(
        _route_kernel,
        grid=(T // tm,),
        in_specs=[pl.BlockSpec((tm, LANES), lambda i: (i, 0))],
        out_specs=[
            pl.BlockSpec((tm, LANES), lambda i: (i, 0)),
            pl.BlockSpec((tm, LANES), lambda i: (i, 0)),
            pl.BlockSpec((8, LANES), lambda i: (0, 0)),
        ],
        out_shape=[
            jax.ShapeDtypeStruct((T, LANES), jnp.int32),
            jax.ShapeDtypeStruct((T, LANES), F32),
            jax.ShapeDtypeStruct((8, LANES), F32),
        ],
        scratch_shapes=[pltpu.VMEM((8, LANES), F32)],
        compiler_params=_cparams(("arbitrary",)),
        name="route",
    )(logits)


DISPATCH_TOKENS = 256


def _dispatch_kernel(dest_ref, h_ref, xs_in_ref, xs_ref, sem):
    del xs_in_ref

    def row_copy(r, k):
        return pltpu.make_async_copy(
            h_ref.at[pl.ds(r, 1)], xs_ref.at[pl.ds(dest_ref[r * TOP_K + k], 1)], sem)

    for r in range(DISPATCH_TOKENS):
        for k in range(TOP_K):
            row_copy(r, k).start(priority=k % 2)
    for r in range(DISPATCH_TOKENS):
        for k in range(TOP_K):
            row_copy(r, k).wait()


def _dispatch(dest_flat, h2, xs_zero):
    T, D = h2.shape
    n = DISPATCH_TOKENS * TOP_K
    return pl.pallas_call(
        _dispatch_kernel,
        grid=(T // DISPATCH_TOKENS,),
        in_specs=[
            pl.BlockSpec((n,), lambda i: (i,), memory_space=pltpu.SMEM),
            pl.BlockSpec((DISPATCH_TOKENS, D), lambda i: (i, 0)),
            pl.BlockSpec(memory_space=pl.ANY),
        ],
        out_specs=pl.BlockSpec(memory_space=pl.ANY),
        out_shape=jax.ShapeDtypeStruct(xs_zero.shape, xs_zero.dtype),
        scratch_shapes=[pltpu.SemaphoreType.DMA(())],
        input_output_aliases={2: 0},
        compiler_params=pltpu.CompilerParams(
            dimension_semantics=("arbitrary",), has_side_effects=True),
        name="dispatch",
    )(dest_flat, h2, xs_zero)


def _expert_starts(be_ref, b):
    return jnp.logical_or(b == 0, be_ref[b] != be_ref[jnp.maximum(b - 1, 0)])


UP_TILE = 1024
DOWN_TILE = 1024
DEINTERLEAVE_W = 256


def _weight_tile_stream(be_ref, nxt_ref, w_hbm, stage, sem, width, on_arrival):
    j = pl.program_id(0)
    b = pl.program_id(1)

    def tile_copy(e, jj):
        col = pl.multiple_of(jj * width, width)
        return pltpu.make_async_copy(w_hbm.at[e, :, pl.ds(col, width)], stage, sem)

    @pl.when(jnp.logical_and(j == 0, b == 0))
    def _():
        tile_copy(be_ref[0], 0).start()

    tile_copy(be_ref[b], j).wait()
    on_arrival()
    nb = nxt_ref[b]
    more_experts = nb >= 0
    e_next = be_ref[jnp.where(more_experts, nb, 0)]
    j_next = jnp.where(more_experts, j, j + 1)

    @pl.when(jnp.logical_or(more_experts, j + 1 < pl.num_programs(0)))
    def _():
        tile_copy(e_next, j_next).start()


def _moe_up_kernel(be_ref, nu_ref, nxt_ref, x_ref, w_hbm, bg_ref, bl_ref, act_ref,
                   stage, wg_scr, wl_scr, sem):
    b = pl.program_id(1)
    active = b < nu_ref[0]

    def deinterleave():
        src = lax.broadcasted_iota(jnp.int32, (DEINTERLEAVE_W, DEINTERLEAVE_W), 0)
        dst = lax.broadcasted_iota(jnp.int32, (DEINTERLEAVE_W, DEINTERLEAVE_W), 1)
        half = DEINTERLEAVE_W // 2
        want = jnp.where(dst < half, 2 * dst, 2 * (dst - half) + 1)
        sel = (src == want).astype(BF16)
        for g in range(2 * UP_TILE // DEINTERLEAVE_W):
            chunk = stage[:, g * DEINTERLEAVE_W:(g + 1) * DEINTERLEAVE_W].astype(BF16)
            moved = _dot(chunk, sel).astype(BF16)
            wg_scr[:, g * half:(g + 1) * half] = moved[:, :half]
            wl_scr[:, g * half:(g + 1) * half] = moved[:, half:]

    @pl.when(jnp.logical_and(active, _expert_starts(be_ref, b)))
    def _():
        _weight_tile_stream(be_ref, nxt_ref, w_hbm, stage, sem, 2 * UP_TILE, deinterleave)

    @pl.when(active)
    def _():
        words = x_ref[...]
        x_hi = pltpu.bitcast(words & jnp.uint32(0xFFFF0000), F32).astype(BF16)
        x_lo = pltpu.bitcast(words << 16, F32).astype(BF16)
        x = jnp.concatenate([x_hi, x_lo], axis=1)
        gate = jnp.minimum(_dot(x, wg_scr[...]) + bg_ref[...], SWIGLU_LIMIT)
        lin = jnp.clip(_dot(x, wl_scr[...]) + bl_ref[...], -SWIGLU_LIMIT, SWIGLU_LIMIT)
        act = gate * jax.nn.sigmoid(SWIGLU_ALPHA * gate) * (lin + 1.0)
        act_ref[...] = act.astype(act_ref.dtype)

    @pl.when(jnp.logical_not(active))
    def _():
        act_ref[...] = jnp.zeros_like(act_ref)


def _moe_up(block_expert, n_used, next_start, xs, w_gate_up, b_gate, b_lin):
    P = xs.shape[0]
    D = w_gate_up.shape[1]
    dff = w_gate_up.shape[2] // 2
    tn = UP_TILE
    nb = P // MOE_ROWS

    def blk(b, nu):
        return jnp.minimum(b, nu[0] - 1)

    grid_spec = pltpu.PrefetchScalarGridSpec(
        num_scalar_prefetch=3,
        grid=(dff // tn, nb),
        in_specs=[
            pl.BlockSpec((MOE_ROWS, D // 2), lambda j, b, be, nu, nx: (blk(b, nu), 0)),
            pl.BlockSpec(memory_space=pl.ANY),
            pl.BlockSpec((None, 1, tn), lambda j, b, be, nu, nx: (be[blk(b, nu)], 0, j)),
            pl.BlockSpec((None, 1, tn), lambda j, b, be, nu, nx: (be[blk(b, nu)], 0, j)),
        ],
        out_specs=pl.BlockSpec((MOE_ROWS, tn), lambda j, b, be, nu, nx: (b, j)),
        scratch_shapes=[
            pltpu.VMEM((D, 2 * tn), F32),
            pltpu.VMEM((D, tn), BF16),
            pltpu.VMEM((D, tn), BF16),
            pltpu.SemaphoreType.DMA(()),
        ],
    )
    return pl.pallas_call(
        _moe_up_kernel,
        grid_spec=grid_spec,
        out_shape=jax.ShapeDtypeStruct((P, dff), BF16),
        compiler_params=_cparams(("arbitrary", "arbitrary")),
        name="moe_up",
    )(block_expert, n_used, next_start, xs, w_gate_up, b_gate, b_lin)


def _moe_down_kernel(be_ref, nu_ref, nxt_ref, a_ref, w_hbm, bd_ref, y_ref, stage, wd_scr, sem):
    b = pl.program_id(1)
    active = b < nu_ref[0]

    def to_bf16():
        wd_scr[...] = stage[...].astype(BF16)

    @pl.when(jnp.logical_and(active, _expert_starts(be_ref, b)))
    def _():
        _weight_tile_stream(be_ref, nxt_ref, w_hbm, stage, sem, DOWN_TILE, to_bf16)

    @pl.when(active)
    def _():
        y_ref[...] = _dot(a_ref[...], wd_scr[...]) + bd_ref[...]

    @pl.when(jnp.logical_not(active))
    def _():
        y_ref[...] = jnp.zeros_like(y_ref)


def _moe_down(block_expert, n_used, next_start, act, w_down, b_down):
    P, dff = act.shape
    D = w_down.shape[2]
    tn = DOWN_TILE
    nb = P // MOE_ROWS

    def blk(b, nu):
        return jnp.minimum(b, nu[0] - 1)

    grid_spec = pltpu.PrefetchScalarGridSpec(
        num_scalar_prefetch=3,
        grid=(D // tn, nb),
        in_specs=[
            pl.BlockSpec((MOE_ROWS, dff), lambda j, b, be, nu, nx: (blk(b, nu), 0)),
            pl.BlockSpec(memory_space=pl.ANY),
            pl.BlockSpec((None, 1, tn), lambda j, b, be, nu, nx: (be[blk(b, nu)], 0, j)),
        ],
        out_specs=pl.BlockSpec((MOE_ROWS, tn), lambda j, b, be, nu, nx: (b, j)),
        scratch_shapes=[
            pltpu.VMEM((dff, tn), F32),
            pltpu.VMEM((dff, tn), BF16),
            pltpu.SemaphoreType.DMA(()),
        ],
    )
    return pl.pallas_call(
        _moe_down_kernel,
        grid_spec=grid_spec,
        out_shape=jax.ShapeDtypeStruct((P, D), F32),
        compiler_params=_cparams(("arbitrary", "arbitrary")),
        name="moe_down",
    )(block_expert, n_used, next_start, act, w_down, b_down)


COMBINE_TOKENS = 256


def _combine_kernel(dest_ref, y_ref, x2_ref, rw_ref, o_ref, buf, sem):
    def row_copy(r, k):
        return pltpu.make_async_copy(
            y_ref.at[pl.ds(dest_ref[r * TOP_K + k], 1)], buf.at[k, pl.ds(r, 1)], sem)

    for r in range(COMBINE_TOKENS):
        for k in range(TOP_K):
            row_copy(r, k).start(priority=k % 2)
    for r in range(COMBINE_TOKENS):
        for k in range(TOP_K):
            row_copy(r, k).wait()
    acc = x2_ref[...]
    for k in range(TOP_K):
        acc = acc + rw_ref[:, k:k + 1] * buf[k]
    o_ref[...] = acc


def _combine(dest_flat, y, x2, route_w):
    T, D = x2.shape
    tc = COMBINE_TOKENS
    return pl.pallas_call(
        _combine_kernel,
        grid=(T // tc,),
        in_specs=[
            pl.BlockSpec((tc * TOP_K,), lambda i: (i,), memory_space=pltpu.SMEM),
            pl.BlockSpec(memory_space=pl.ANY),
            pl.BlockSpec((tc, D), lambda i: (i, 0)),
            pl.BlockSpec((tc, LANES), lambda i: (i, 0)),
        ],
        out_specs=pl.BlockSpec((tc, D), lambda i: (i, 0)),
        out_shape=jax.ShapeDtypeStruct((T, D), F32),
        scratch_shapes=[pltpu.VMEM((TOP_K, tc, D), F32), pltpu.SemaphoreType.DMA(())],
        compiler_params=_cparams(("arbitrary",)),
        name="combine",
    )(dest_flat, y, x2, route_w)


def _pad_lanes(v, offset):
    return jnp.zeros((LANES,), F32).at[offset:offset + v.shape[0]].set(v.astype(F32))


def _layer(x, norm1_w, w_in, b_fox_f, fox_q_norm_w, fox_k_norm_w, gdn_conv_w, gdn_A_log,
           gdn_dt_bias, gdn_out_norm_w, w_up_fox, w_up_gdn, w_o, norm2_w, w_router, b_router,
           w_gate_up, b_gate_up, w_down, b_down):
    B, S, D = x.shape
    T = B * S
    x2d = x.reshape(T, D)

    o_f = 3 * HEADS_W
    o_gqkv = o_f + N_HEADS
    o_gb = o_gqkv + 4 * HEADS_W
    o_gate = o_gb + 2 * N_HEADS
    w_big = jnp.concatenate(
        [w_in[:, o_gate:], w_in[:, :o_f], w_in[:, o_gqkv:o_gb]], axis=1).astype(BF16)
    w_small = jnp.concatenate(
        [w_in[:, o_f:o_gqkv], w_in[:, o_gb:o_gate],
         jnp.zeros((D, LANES - 3 * N_HEADS), w_in.dtype)], axis=1).astype(BF16)

    proj, small, v_t = _in_proj(x2d, norm1_w.reshape(1, D), w_big, w_small,
                                fox_q_norm_w.reshape(1, HEAD_DIM), fox_k_norm_w.reshape(1, HEAD_DIM))

    gate_params = jnp.zeros((8, LANES), F32)
    gate_params = gate_params.at[0].set(_pad_lanes(b_fox_f, LANE_C) + _pad_lanes(gdn_dt_bias, LANE_GC))
    gate_params = gate_params.at[1].set(_pad_lanes(gdn_A_log, LANE_GC))
    gates = _gates(small, gate_params, S)
    gates_t = gates[:, :32].T
    c_bcast = jnp.broadcast_to(
        (gates_t[LANE_C:LANE_C + N_HEADS] * LOG2E)[:, :, None], (N_HEADS, T, LANES))

    y_fox = _fox(proj, v_t, c_bcast, B, S)

    u, w, qd, at, kdt, gl = _gdn_prep(proj, gdn_conv_w.astype(F32), gates, gates_t, S)
    y_gdn = _gdn_scan(u, w, qd, at, kdt, gl, proj, gdn_out_norm_w.reshape(1, HEAD_DIM), B, S)

    w_router_p = jnp.zeros((D, LANES), F32).at[:, :N_EXPERTS].set(w_router.astype(F32))
    w_router_hi = w_router_p.astype(BF16)
    w_router_lo = (w_router_p - w_router_hi.astype(F32)).astype(BF16)
    w_router_split = jnp.concatenate([w_router_hi, w_router_lo], axis=1)
    b_router_p = _pad_lanes(b_router, 0).reshape(1, LANES)
    x2, h2, logits = _merge(y_fox, y_gdn, proj, x2d, w_up_fox.astype(BF16), w_up_gdn.astype(BF16),
                            w_o.astype(BF16), norm2_w.reshape(1, D), w_router_split, b_router_p)

    route_i, route_w, counts = _route(logits)
    expert = route_i[:, :TOP_K]
    rank = route_i[:, LANE_RANK:LANE_RANK + TOP_K]
    counts = counts[0, :N_EXPERTS].astype(jnp.int32)
    padded = (counts + MOE_ROWS - 1) // MOE_ROWS * MOE_ROWS
    pad_end = jnp.cumsum(padded)
    pad_start = pad_end - padded
    dest_flat = (pad_start[expert] + rank).reshape(T * TOP_K).astype(jnp.int32)
    n_blocks = -(-T * TOP_K // MOE_ROWS) + N_EXPERTS
    n_used = (pad_end[-1:] // MOE_ROWS).astype(jnp.int32)
    block_start = jnp.arange(n_blocks, dtype=jnp.int32) * MOE_ROWS
    block_expert = jnp.minimum(
        jnp.sum((pad_end[None, :] <= block_start[:, None]).astype(jnp.int32), axis=1),
        N_EXPERTS - 1).astype(jnp.int32)

    xs = _dispatch(dest_flat, h2, jnp.zeros((n_blocks * MOE_ROWS, h2.shape[1]), h2.dtype))

    dff = w_down.shape[1]
    b_gu = b_gate_up.reshape(N_EXPERTS, 1, dff, 2).astype(F32)
    after = (pad_end[block_expert] // MOE_ROWS).astype(jnp.int32)
    next_start = jnp.where(after < n_used[0], after, -1).astype(jnp.int32)
    act = _moe_up(block_expert, n_used, next_start, xs, w_gate_up.astype(F32),
                  b_gu[..., 0], b_gu[..., 1])
    y = _moe_down(block_expert, n_used, next_start, act, w_down.astype(F32),
                  b_down.reshape(N_EXPERTS, 1, D).astype(F32))

    out = _combine(dest_flat, y, x2, route_w)
    return out.reshape(B, S, D)


def kernel(x, norm1_w, w_in, b_fox_f, fox_q_norm_w, fox_k_norm_w, gdn_conv_w, gdn_A_log, gdn_dt_bias, gdn_out_norm_w, w_up_fox, w_up_gdn, w_o, norm2_w, w_router, b_router, w_gate_up, b_gate_up, w_down, b_down):
    depth = norm1_w.shape[0]
    for l in range(depth):
        x = _layer(x, norm1_w[l], w_in[l], b_fox_f[l], fox_q_norm_w[l], fox_k_norm_w[l],
                   gdn_conv_w[l], gdn_A_log[l], gdn_dt_bias[l], gdn_out_norm_w[l], w_up_fox[l],
                   w_up_gdn[l], w_o[l], norm2_w[l], w_router[l], b_router[l], w_gate_up[l],
                   b_gate_up[l], w_down[l], b_down[l])
    return x
```
